```python
import jax, jax.numpy as jnp
from jax import lax
import numpy as np

D_MODEL = 1024
BATCH = 8
SEQ = 2048
DEPTH = 4
DEC_BATCH = 128
DEC_SEQ = 8
PAST_LEN = 8192
PAGE_SIZE = 128

N_HEADS = 8
HEAD_DIM = 64
N_KV_HEADS = 2
GROUP = N_HEADS // N_KV_HEADS
ATTN_DIM = N_HEADS * HEAD_DIM
KV_DIM = N_KV_HEADS * HEAD_DIM
WINDOW = 128
Q_BLOCK = 128
CONV_DIM = D_MODEL // 4
CONV_K = 31
POOL_DIM = D_MODEL // 4
POOL_WINDOWS = (2, 4, 8, 16)
N_POOL_GROUPS = 4
POOL_GROUP_DIM = POOL_DIM // N_POOL_GROUPS
POOL_MAX = 16
IN_DIM = ATTN_DIM + 2 * KV_DIM + 2 * CONV_DIM + POOL_DIM
MIX_DIM = ATTN_DIM + CONV_DIM + POOL_DIM
D_FF = ((8 * D_MODEL // 3 + 127) // 128) * 128
RMS_EPS = 1e-6
LN_EPS = 1e-5
NEG_INF = -1e30

kernel_name = "hymba_style_swa_conv_pool_macaron_decoder_step"


def rms_norm(x, g):
    xf = x.astype(jnp.float32)
    y = xf * lax.rsqrt(jnp.mean(xf * xf, axis=-1, keepdims=True) + RMS_EPS)
    return (y * g.astype(jnp.float32)).astype(x.dtype)


def layer_norm(x, g, b):
    xf = x.astype(jnp.float32)
    mu = jnp.mean(xf, axis=-1, keepdims=True)
    xc = xf - mu
    y = xc * lax.rsqrt(jnp.mean(xc * xc, axis=-1, keepdims=True) + LN_EPS)
    return (y * g.astype(jnp.float32) + b.astype(jnp.float32)).astype(x.dtype)


def swiglu(x, wg, wu, wd):
    return (jax.nn.silu(x @ wg) * (x @ wu)) @ wd


def alibi_slopes():
    return jnp.exp2(-8.0 * jnp.arange(1, N_HEADS + 1, dtype=jnp.float32) / N_HEADS)


def window_attention(q, k_ext, v_ext, sinks, pos0):
    B, T = q.shape[0], q.shape[1]
    bq = T if T <= Q_BLOCK else Q_BLOCK
    nb = T // bq
    span = WINDOW + bq
    kidx = jnp.arange(nb)[:, None] * bq + jnp.arange(span)[None, :]
    kb = k_ext[:, kidx]
    vb = v_ext[:, kidx]
    qb = q.reshape(B, nb, bq, N_KV_HEADS, GROUP, HEAD_DIM)
    scale = HEAD_DIM ** -0.5
    logits = jnp.einsum('bnqkgd,bnskd->bnkgqs', qb.astype(jnp.float32),
                        kb.astype(jnp.float32)) * scale
    t_pos = pos0 + jnp.arange(nb)[:, None] * bq + jnp.arange(bq)[None, :]
    s_pos = pos0 - WINDOW + kidx
    dist = t_pos[:, :, None] - s_pos[:, None, :]
    valid = (dist >= 0) & (dist < WINDOW) & (s_pos[:, None, :] >= 0)
    slopes = alibi_slopes().reshape(N_KV_HEADS, GROUP)
    bias = -slopes[None, :, :, None, None] * dist[:, None, None, :, :].astype(jnp.float32)
    logits = jnp.where(valid[:, None, None], logits + bias, NEG_INF)
    sink = sinks.astype(jnp.float32).reshape(N_KV_HEADS, GROUP)[None, None, :, :, None, None]
    m = jnp.maximum(jnp.max(logits, axis=-1, keepdims=True), sink)
    e = jnp.exp(logits - m)
    p = e / (jnp.sum(e, axis=-1, keepdims=True) + jnp.exp(sink - m))
    out = jnp.einsum('bnkgqs,bnskd->bnqkgd', p.astype(v_ext.dtype), vb)
    return out.reshape(B, T, ATTN_DIM)


def conv_module(g_ext, dw_w, dw_b, ln_g, ln_b, pw_w):
    c = lax.conv_general_dilated(g_ext, dw_w[:, None, :].astype(g_ext.dtype), (1,), 'VALID',
                                 dimension_numbers=('NWC', 'WIO', 'NWC'),
                                 feature_group_count=CONV_DIM) + dw_b
    c = layer_norm(c, ln_g, ln_b)
    return jax.nn.silu(c) @ pw_w


def pool_mixer(p_ext, pos0, pool_w, pool_scale):
    P = POOL_MAX - 1
    B, T = p_ext.shape[0], p_ext.shape[1] - P
    pf = p_ext.astype(jnp.float32)
    csum = jnp.cumsum(jnp.pad(pf, ((0, 0), (1, 0), (0, 0))), axis=1)
    cur = pf[:, P:]
    pos = pos0 + jnp.arange(T)
    diffs = []
    for g, w in enumerate(POOL_WINDOWS):
        sl = slice(g * POOL_GROUP_DIM, (g + 1) * POOL_GROUP_DIM)
        wsum = csum[:, P + 1:P + 1 + T, sl] - csum[:, P + 1 - w:P + 1 - w + T, sl]
        cnt = jnp.minimum(w, pos + 1).astype(jnp.float32)[None, :, None]
        diffs.append(wsum / cnt - cur[:, :, sl])
    d = jnp.stack(diffs, axis=2)
    out = jnp.einsum('btgc,gcd->btgd', d, pool_w.astype(jnp.float32)).reshape(B, T, POOL_DIM)
    return (out * pool_scale.astype(jnp.float32)).astype(p_ext.dtype)


def trunk_layer(x, k_buf, v_buf, conv_buf, pool_buf, pos0,
                norm_g, f1g, f1u, f1d, w_in, sinks, dw_w, dw_b, ln_g, ln_b, pw_w,
                pool_w, pool_scale, w_out, f2g, f2u, f2d):
    B, T = x.shape[0], x.shape[1]
    h = x + 0.5 * rms_norm(swiglu(rms_norm(x, norm_g[0]), f1g, f1u, f1d), norm_g[1])
    u = rms_norm(h, norm_g[2])
    z = u @ w_in
    o = 0
    q = z[..., o:o + ATTN_DIM].reshape(B, T, N_HEADS, HEAD_DIM); o += ATTN_DIM
    k = z[..., o:o + KV_DIM].reshape(B, T, N_KV_HEADS, HEAD_DIM); o += KV_DIM
    v = z[..., o:o + KV_DIM].reshape(B, T, N_KV_HEADS, HEAD_DIM); o += KV_DIM
    ga = z[..., o:o + CONV_DIM]; o += CONV_DIM
    gb = z[..., o:o + CONV_DIM]; o += CONV_DIM
    pu = z[..., o:o + POOL_DIM]
    k_ext = jnp.concatenate([k_buf, k], axis=1)
    v_ext = jnp.concatenate([v_buf, v], axis=1)
    attn = window_attention(q, k_ext, v_ext, sinks, pos0)
    g_ext = jnp.concatenate([conv_buf, ga * jax.nn.sigmoid(gb)], axis=1)
    conv = conv_module(g_ext, dw_w, dw_b, ln_g, ln_b, pw_w)
    p_ext = jnp.concatenate([pool_buf, pu], axis=1)
    pool = pool_mixer(p_ext, pos0, pool_w, pool_scale)
    mix = jnp.concatenate([attn, conv, pool], axis=-1) @ w_out
    h = h + rms_norm(mix, norm_g[3])
    y = h + 0.5 * rms_norm(swiglu(rms_norm(h, norm_g[4]), f2g, f2u, f2d), norm_g[5])
    return (y, k_ext[:, -WINDOW:], v_ext[:, -WINDOW:],
            g_ext[:, -(CONV_K - 1):], p_ext[:, -(POOL_MAX - 1):])


def setup_inputs(seed: int = 0) -> dict:
    key = jax.random.key(seed)
    ks = jax.random.split(key, 24)
    nrm = lambda k, s, sc: jax.random.normal(k, s, jnp.float32) * sc
    return {
        "x_prompt": nrm(ks[0], (BATCH, SEQ, D_MODEL), 1.0),
        "x_sample": nrm(ks[1], (DEC_BATCH, DEC_SEQ, D_MODEL), 1.0),
        "state_attn_k": nrm(ks[2], (DEPTH, DEC_BATCH, WINDOW, N_KV_HEADS, HEAD_DIM), 1.0),
        "state_attn_v": nrm(ks[3], (DEPTH, DEC_BATCH, WINDOW, N_KV_HEADS, HEAD_DIM), 1.0),
        "state_conv": nrm(ks[4], (DEPTH, DEC_BATCH, CONV_K - 1, CONV_DIM), 0.5),
        "state_pool": nrm(ks[5], (DEPTH, DEC_BATCH, POOL_MAX - 1, POOL_DIM), 1.0),
        "norm_g": 1.0 + nrm(ks[6], (DEPTH, 6, D_MODEL), 0.02),
        "ffn1_wg": nrm(ks[7], (DEPTH, D_MODEL, D_FF), D_MODEL ** -0.5),
        "ffn1_wu": nrm(ks[8], (DEPTH, D_MODEL, D_FF), D_MODEL ** -0.5),
        "ffn1_wd": nrm(ks[9], (DEPTH, D_FF, D_MODEL), D_FF ** -0.5),
        "w_in": nrm(ks[10], (DEPTH, D_MODEL, IN_DIM), D_MODEL ** -0.5),
        "attn_sinks": nrm(ks[11], (DEPTH, N_HEADS), 0.5),
        "conv_dw_w": nrm(ks[12], (DEPTH, CONV_K, CONV_DIM), CONV_K ** -0.5),
        "conv_dw_b": nrm(ks[13], (DEPTH, CONV_DIM), 0.02),
        "conv_ln_g": 1.0 + nrm(ks[14], (DEPTH, CONV_DIM), 0.02),
        "conv_ln_b": nrm(ks[15], (DEPTH, CONV_DIM), 0.02),
        "conv_pw_w": nrm(ks[16], (DEPTH, CONV_DIM, CONV_DIM), CONV_DIM ** -0.5),
        "pool_w": nrm(ks[17], (DEPTH, N_POOL_GROUPS, POOL_GROUP_DIM, POOL_GROUP_DIM), POOL_GROUP_DIM ** -0.5),
        "pool_scale": 1.0 + nrm(ks[18], (DEPTH, POOL_DIM), 0.1),
        "w_out": nrm(ks[19], (DEPTH, MIX_DIM, D_MODEL), MIX_DIM ** -0.5),
        "ffn2_wg": nrm(ks[20], (DEPTH, D_MODEL, D_FF), D_MODEL ** -0.5),
        "ffn2_wu": nrm(ks[21], (DEPTH, D_MODEL, D_FF), D_MODEL ** -0.5),
        "ffn2_wd": nrm(ks[22], (DEPTH, D_FF, D_MODEL), D_FF ** -0.5),
    }


def reference(x_prompt, x_sample, state_attn_k, state_attn_v, state_conv, state_pool,
              norm_g, ffn1_wg, ffn1_wu, ffn1_wd, w_in, attn_sinks, conv_dw_w, conv_dw_b,
              conv_ln_g, conv_ln_b, conv_pw_w, pool_w, pool_scale, w_out,
              ffn2_wg, ffn2_wu, ffn2_wd):
    params = (norm_g, ffn1_wg, ffn1_wu, ffn1_wd, w_in, attn_sinks, conv_dw_w, conv_dw_b,
              conv_ln_g, conv_ln_b, conv_pw_w, pool_w, pool_scale, w_out,
              ffn2_wg, ffn2_wu, ffn2_wd)
    dt = x_prompt.dtype
    zk = jnp.zeros((BATCH, WINDOW, N_KV_HEADS, HEAD_DIM), dt)
    zc = jnp.zeros((BATCH, CONV_K - 1, CONV_DIM), dt)
    zp = jnp.zeros((BATCH, POOL_MAX - 1, POOL_DIM), dt)
    hp, hs = x_prompt, x_sample
    kp, vp, cp, pp, ksl, vsl, csl, psl = [], [], [], [], [], [], [], []
    for l in range(DEPTH):
        lp = [p[l] for p in params]
        hp, k1, v1, c1, p1 = trunk_layer(hp, zk, zk, zc, zp, 0, *lp)
        hs, k2, v2, c2, p2 = trunk_layer(hs, state_attn_k[l], state_attn_v[l],
                                         state_conv[l], state_pool[l], PAST_LEN, *lp)
        kp.append(k1); vp.append(v1); cp.append(c1); pp.append(p1)
        ksl.append(k2); vsl.append(v2); csl.append(c2); psl.append(p2)
    return (hp, hs,
            jnp.stack(kp), jnp.stack(vp), jnp.stack(cp), jnp.stack(pp),
            jnp.stack(ksl), jnp.stack(vsl), jnp.stack(csl), jnp.stack(psl))
```

```python
import functools

import numpy as np
import jax
import jax.numpy as jnp
from jax import lax
from jax.experimental import pallas as pl
from jax.experimental.pallas import tpu as pltpu

D_MODEL = 1024
BATCH = 8
SEQ = 2048
DEPTH = 4
DEC_BATCH = 128
DEC_SEQ = 8
PAST_LEN = 8192

N_HEADS = 8
HEAD_DIM = 64
N_KV_HEADS = 2
GROUP = N_HEADS // N_KV_HEADS
ATTN_DIM = N_HEADS * HEAD_DIM
KV_DIM = N_KV_HEADS * HEAD_DIM
WINDOW = 128
CONV_DIM = D_MODEL // 4
CONV_K = 31
POOL_DIM = D_MODEL // 4
POOL_WINDOWS = (2, 4, 8, 16)
N_POOL_GROUPS = 4
POOL_GROUP_DIM = POOL_DIM // N_POOL_GROUPS
POOL_MAX = 16
IN_DIM = ATTN_DIM + 2 * KV_DIM + 2 * CONV_DIM + POOL_DIM
MIX_DIM = ATTN_DIM + CONV_DIM + POOL_DIM
D_FF = ((8 * D_MODEL // 3 + 127) // 128) * 128
RMS_EPS = 1e-6
LN_EPS = 1e-5
NEG_INF = -1e30

Q_OFF = 0
K_OFF = ATTN_DIM
V_OFF = K_OFF + KV_DIM
GA_OFF = V_OFF + KV_DIM
GB_OFF = GA_OFF + CONV_DIM
PU_OFF = GB_OFF + CONV_DIM
CONV_OUT_OFF = ATTN_DIM
POOL_OUT_OFF = ATTN_DIM + CONV_DIM

LANES = 128
FF_CHUNK = D_FF // 2
TM_FFN = 256
TM_MIX = 256
Q_BLK = 128
CONV_ROWS = 64
CONV_HALO = 32
POOL_HALO = 16
SEQ_BLK = 8
S_KEYS = 144
VMEM_LIMIT = 48 * 1024 * 1024

ALIBI_SLOPES = [float(s) for s in np.exp2(-8.0 * np.arange(1, N_HEADS + 1, dtype=np.float32) / N_HEADS)]
Q_SCALE = HEAD_DIM ** -0.5

F32 = jnp.float32
BF16 = jnp.bfloat16


def _rms(x, g):
    return x * lax.rsqrt(jnp.mean(x * x, axis=-1, keepdims=True) + RMS_EPS) * g


def _swiglu(xn, wg_ref, wu_ref, wd_ref):
    acc = None
    for c in range(D_FF // FF_CHUNK):
        sl = slice(c * FF_CHUNK, (c + 1) * FF_CHUNK)
        g = jnp.dot(xn, wg_ref[:, sl], preferred_element_type=F32)
        u = jnp.dot(xn, wu_ref[:, sl], preferred_element_type=F32)
        a = (g * jax.nn.sigmoid(g) * u).astype(BF16)
        part = jnp.dot(a, wd_ref[sl, :], preferred_element_type=F32)
        acc = part if acc is None else acc + part
    return acc


def _ffn_in_kernel(x_ref, ng_ref, wg_ref, wu_ref, wd_ref, win_ref, h_ref, z_ref):
    x = x_ref[...]
    xn = _rms(x, ng_ref[0:1, :]).astype(BF16)
    h = x + 0.5 * _rms(_swiglu(xn, wg_ref, wu_ref, wd_ref), ng_ref[1:2, :])
    h_ref[...] = h
    u = _rms(h, ng_ref[2:3, :]).astype(BF16)
    z_ref[...] = jnp.dot(u, win_ref[...], preferred_element_type=F32)


def _out_ffn_kernel(h_ref, mix_ref, ng_ref, wout_ref, wg_ref, wu_ref, wd_ref, y_ref):
    m = jnp.dot(mix_ref[...].astype(BF16), wout_ref[...], preferred_element_type=F32)
    h = h_ref[...] + _rms(m, ng_ref[3:4, :])
    hn = _rms(h, ng_ref[4:5, :]).astype(BF16)
    y_ref[...] = h + 0.5 * _rms(_swiglu(hn, wg_ref, wu_ref, wd_ref), ng_ref[5:6, :])


def _const_spec(shape):
    return pl.BlockSpec(shape, lambda *_: (0,) * len(shape), pipeline_mode=pl.Buffered(1))


def _ffn_in(x, ng, wg, wu, wd, win):
    n = x.shape[0]
    row = lambda i: (i, 0)
    return pl.pallas_call(
        _ffn_in_kernel,
        grid=(n // TM_FFN,),
        in_specs=[pl.BlockSpec((TM_FFN, D_MODEL), row),
                  _const_spec((6, D_MODEL)),
                  _const_spec((D_MODEL, D_FF)), _const_spec((D_MODEL, D_FF)),
                  _const_spec((D_FF, D_MODEL)), _const_spec((D_MODEL, IN_DIM))],
        out_specs=[pl.BlockSpec((TM_FFN, D_MODEL), row), pl.BlockSpec((TM_FFN, IN_DIM), row)],
        out_shape=[jax.ShapeDtypeStruct((n, D_MODEL), F32), jax.ShapeDtypeStruct((n, IN_DIM), F32)],
        compiler_params=pltpu.CompilerParams(dimension_semantics=("arbitrary",),
                                             vmem_limit_bytes=VMEM_LIMIT),
        name="ffn_in",
    )(x, ng, wg, wu, wd, win)


def _out_ffn(h, mix, ng, wout, wg, wu, wd):
    n = h.shape[0]
    row = lambda i: (i, 0)
    return pl.pallas_call(
        _out_ffn_kernel,
        grid=(n // TM_FFN,),
        in_specs=[pl.BlockSpec((TM_FFN, D_MODEL), row), pl.BlockSpec((TM_FFN, MIX_DIM), row),
                  _const_spec((6, D_MODEL)), _const_spec((MIX_DIM, D_MODEL)),
                  _const_spec((D_MODEL, D_FF)), _const_spec((D_MODEL, D_FF)),
                  _const_spec((D_FF, D_MODEL))],
        out_specs=pl.BlockSpec((TM_FFN, D_MODEL), row),
        out_shape=jax.ShapeDtypeStruct((n, D_MODEL), F32),
        compiler_params=pltpu.CompilerParams(dimension_semantics=("arbitrary",),
                                             vmem_limit_bytes=VMEM_LIMIT),
        name="out_ffn",
    )(h, mix, ng, wout, wg, wu, wd)


def _stack_q_heads(q):
    lane = lax.broadcasted_iota(jnp.int32, q.shape[:-1] + (LANES,), q.ndim - 1)
    pieces = []
    for h in range(N_HEADS):
        kvh = h // GROUP
        slab = q[..., (h // 2) * LANES:(h // 2 + 1) * LANES]
        if h % 2 != kvh:
            slab = pltpu.roll(slab, HEAD_DIM, q.ndim - 1)
        keep = (lane >= kvh * HEAD_DIM) & (lane < (kvh + 1) * HEAD_DIM)
        pieces.append((jnp.where(keep, slab, 0.0) * Q_SCALE).astype(BF16))
    return jnp.concatenate(pieces, axis=-2)


def _unstack_heads(o_heads):
    ndim = o_heads[0].ndim
    lane = lax.broadcasted_iota(jnp.int32, o_heads[0].shape, ndim - 1)
    slabs = []
    for j in range(N_HEADS // 2):
        a, b = o_heads[2 * j], o_heads[2 * j + 1]
        kvh = (2 * j) // GROUP
        if kvh == 0:
            b = pltpu.roll(b, HEAD_DIM, ndim - 1)
        else:
            a = pltpu.roll(a, HEAD_DIM, ndim - 1)
        slabs.append(jnp.where(lane < HEAD_DIM, a, b))
    return jnp.concatenate(slabs, axis=-1)


def _softmax_heads(s, valid, dist_f, sinks_ref, rows):
    es, rinv = [], []
    for h in range(N_HEADS):
        sink = sinks_ref[h]
        sh = s[..., h * rows:(h + 1) * rows, :]
        l = jnp.where(valid, sh - ALIBI_SLOPES[h] * dist_f, NEG_INF)
        m = jnp.maximum(jnp.max(l, axis=-1, keepdims=True), sink)
        e = jnp.exp(l - m)
        den = jnp.sum(e, axis=-1, keepdims=True) + jnp.exp(sink - m)
        es.append(e.astype(BF16))
        rinv.append(1.0 / den)
    return es, rinv


def _conv_tail(c, lng_ref, lnb_ref, pw_ref):
    mu = jnp.mean(c, axis=-1, keepdims=True)
    xc = c - mu
    y = xc * lax.rsqrt(jnp.mean(xc * xc, axis=-1, keepdims=True) + LN_EPS) * lng_ref[...] + lnb_ref[...]
    y = (y * jax.nn.sigmoid(y)).astype(BF16)
    return jnp.dot(y, pw_ref[...], preferred_element_type=F32)


def _pool_tail(sums, cnt, cur, poolw_ref, pools_ref):
    lane = lax.broadcasted_iota(jnp.int32, cur.shape, cur.ndim - 1)
    wsum = sums[-1]
    for g in range(N_POOL_GROUPS - 2, -1, -1):
        wsum = jnp.where(lane < (g + 1) * POOL_GROUP_DIM, sums[g], wsum)
    d = (wsum / cnt - cur).astype(BF16)
    return jnp.dot(d, poolw_ref[...], preferred_element_type=F32) * pools_ref[...]


def _pool_window_lane():
    lane = lax.broadcasted_iota(jnp.int32, (1, POOL_DIM), 1)
    w = jnp.full((1, POOL_DIM), POOL_WINDOWS[-1], jnp.int32)
    for g in range(N_POOL_GROUPS - 2, -1, -1):
        w = jnp.where(lane < (g + 1) * POOL_GROUP_DIM, POOL_WINDOWS[g], w)
    return w


def _prompt_mixer_kernel(sinks_ref, z_ref, dww_ref, dwb_ref, lng_ref, lnb_ref, pw_ref, poolw_ref, pools_ref,
                         mix_ref, nk_ref, nv_ref, nc_ref, np_ref,
                         kx_ref, vx_ref, gx_ref, px_ref):
    i = pl.program_id(1)
    tm = TM_MIX

    @pl.when(i == 0)
    def _():
        kx_ref[0:WINDOW, :] = jnp.zeros((WINDOW, KV_DIM), BF16)
        vx_ref[0:WINDOW, :] = jnp.zeros((WINDOW, KV_DIM), BF16)
        gx_ref[0:CONV_HALO, :] = jnp.zeros((CONV_HALO, CONV_DIM), F32)
        px_ref[0:POOL_HALO, :] = jnp.zeros((POOL_HALO, POOL_DIM), F32)

    kx_ref[WINDOW:, :] = z_ref[:, K_OFF:K_OFF + KV_DIM].astype(BF16)
    vx_ref[WINDOW:, :] = z_ref[:, V_OFF:V_OFF + KV_DIM].astype(BF16)
    gx_ref[CONV_HALO:, :] = z_ref[:, GA_OFF:GA_OFF + CONV_DIM] * jax.nn.sigmoid(z_ref[:, GB_OFF:GB_OFF + CONV_DIM])
    px_ref[POOL_HALO:, :] = z_ref[:, PU_OFF:PU_OFF + POOL_DIM]

    q_io = lax.broadcasted_iota(jnp.int32, (Q_BLK, WINDOW + Q_BLK), 0)
    s_io = lax.broadcasted_iota(jnp.int32, (Q_BLK, WINDOW + Q_BLK), 1)
    dist = q_io + WINDOW - s_io
    dist_f = dist.astype(F32)
    band = (dist >= 0) & (dist < WINDOW)
    for blk in range(tm // Q_BLK):
        r0 = blk * Q_BLK
        first_key_pos = i * tm + r0 - WINDOW
        valid = band & (s_io + first_key_pos >= 0)
        qs = _stack_q_heads(z_ref[r0:r0 + Q_BLK, Q_OFF:Q_OFF + ATTN_DIM])
        kb = kx_ref[r0:r0 + WINDOW + Q_BLK, :]
        vb = vx_ref[r0:r0 + WINDOW + Q_BLK, :]
        s = lax.dot_general(qs, kb, (((1,), (1,)), ((), ())), preferred_element_type=F32)
        es, rinv = _softmax_heads(s, valid, dist_f, sinks_ref, Q_BLK)
        o = jnp.dot(jnp.concatenate(es, axis=0), vb, preferred_element_type=F32)
        o_heads = [o[h * Q_BLK:(h + 1) * Q_BLK, :] * rinv[h] for h in range(N_HEADS)]
        mix_ref[r0:r0 + Q_BLK, 0:ATTN_DIM] = _unstack_heads(o_heads).astype(mix_ref.dtype)

    for rc in range(tm // CONV_ROWS):
        r0 = rc * CONV_ROWS
        c = jnp.broadcast_to(dwb_ref[...], (CONV_ROWS, CONV_DIM))
        for j in range(CONV_K):
            off = CONV_HALO - (CONV_K - 1) + j + r0
            c = c + dww_ref[j:j + 1, :] * gx_ref[off:off + CONV_ROWS, :]
        mix_ref[r0:r0 + CONV_ROWS, CONV_OUT_OFF:CONV_OUT_OFF + CONV_DIM] = (
            _conv_tail(c, lng_ref, lnb_ref, pw_ref).astype(mix_ref.dtype))

    w_lane = _pool_window_lane()
    for rc in range(tm // Q_BLK):
        r0 = rc * Q_BLK
        cur = px_ref[POOL_HALO + r0:POOL_HALO + r0 + Q_BLK, :]
        run = cur
        sums = []
        for back in range(1, POOL_MAX):
            run = run + px_ref[POOL_HALO + r0 - back:POOL_HALO + r0 - back + Q_BLK, :]
            if back + 1 in POOL_WINDOWS:
                sums.append(run)
        pos = i * tm + r0 + lax.broadcasted_iota(jnp.int32, (Q_BLK, 1), 0)
        cnt = jnp.minimum(w_lane, pos + 1).astype(F32)
        mix_ref[r0:r0 + Q_BLK, POOL_OUT_OFF:POOL_OUT_OFF + POOL_DIM] = (
            _pool_tail(sums, cnt, cur, poolw_ref, pools_ref).astype(mix_ref.dtype))

    nk_ref[...] = z_ref[tm - WINDOW:, K_OFF:K_OFF + KV_DIM]
    nv_ref[...] = z_ref[tm - WINDOW:, V_OFF:V_OFF + KV_DIM]
    nc_ref[...] = gx_ref[CONV_HALO + tm - (CONV_K - 1):, :]
    np_ref[...] = px_ref[POOL_HALO + tm - (POOL_MAX - 1):, :]

    kx_ref[0:WINDOW, :] = kx_ref[tm:tm + WINDOW, :]
    vx_ref[0:WINDOW, :] = vx_ref[tm:tm + WINDOW, :]
    gx_ref[0:CONV_HALO, :] = gx_ref[tm:tm + CONV_HALO, :]
    px_ref[0:POOL_HALO, :] = px_ref[tm:tm + POOL_HALO, :]


def _prompt_mixer(z, sinks, dww, dwb, lng, lnb, pw, poolw, pools):
    tiles = SEQ // TM_MIX
    tok = lambda b, i: (b * tiles + i, 0)
    seq3 = lambda b, i: (b, 0, 0)
    small = lambda shape: pl.BlockSpec(shape, lambda b, i: (0,) * len(shape))
    return pl.pallas_call(
        _prompt_mixer_kernel,
        grid=(BATCH, tiles),
        in_specs=[pl.BlockSpec(memory_space=pltpu.SMEM),
                  pl.BlockSpec((TM_MIX, IN_DIM), tok),
                  small((CONV_K, CONV_DIM)), small((1, CONV_DIM)), small((1, CONV_DIM)), small((1, CONV_DIM)),
                  small((CONV_DIM, CONV_DIM)), small((POOL_DIM, POOL_DIM)), small((1, POOL_DIM))],
        out_specs=[pl.BlockSpec((TM_MIX, MIX_DIM), tok),
                   pl.BlockSpec((None, WINDOW, KV_DIM), seq3), pl.BlockSpec((None, WINDOW, KV_DIM), seq3),
                   pl.BlockSpec((None, CONV_K - 1, CONV_DIM), seq3),
                   pl.BlockSpec((None, POOL_MAX - 1, POOL_DIM), seq3)],
        out_shape=[jax.ShapeDtypeStruct((BATCH * SEQ, MIX_DIM), BF16),
                   jax.ShapeDtypeStruct((BATCH, WINDOW, KV_DIM), F32),
                   jax.ShapeDtypeStruct((BATCH, WINDOW, KV_DIM), F32),
                   jax.ShapeDtypeStruct((BATCH, CONV_K - 1, CONV_DIM), F32),
                   jax.ShapeDtypeStruct((BATCH, POOL_MAX - 1, POOL_DIM), F32)],
        scratch_shapes=[pltpu.VMEM((WINDOW + TM_MIX, KV_DIM), BF16),
                        pltpu.VMEM((WINDOW + TM_MIX, KV_DIM), BF16),
                        pltpu.VMEM((CONV_HALO + TM_MIX, CONV_DIM), F32),
                        pltpu.VMEM((POOL_HALO + TM_MIX, POOL_DIM), F32)],
        compiler_params=pltpu.CompilerParams(dimension_semantics=("arbitrary", "arbitrary"),
                                             vmem_limit_bytes=VMEM_LIMIT),
        name="prompt_mixer",
    )(sinks, z, dww, dwb, lng, lnb, pw, poolw, pools)


def _sample_mixer_kernel(sinks_ref, z_ref, sk_ref, sv_ref, sc_ref, sp_ref,
                         dww_ref, dwb_ref, lng_ref, lnb_ref, pw_ref, poolw_ref, pools_ref,
                         mix_ref, nk_ref, nv_ref, nc_ref, np_ref,
                         kx_ref, vx_ref, gx_ref, px_ref):
    nb, t = SEQ_BLK, DEC_SEQ
    k_new = z_ref[:, :, K_OFF:K_OFF + KV_DIM]
    v_new = z_ref[:, :, V_OFF:V_OFF + KV_DIM]

    kx_ref[:, 0:WINDOW, :] = sk_ref[...].astype(BF16)
    vx_ref[:, 0:WINDOW, :] = sv_ref[...].astype(BF16)
    pad_rows = S_KEYS - WINDOW
    kx_ref[:, WINDOW:, :] = jnp.concatenate(
        [k_new, jnp.zeros((nb, pad_rows - t, KV_DIM), F32)], axis=1).astype(BF16)
    vx_ref[:, WINDOW:, :] = jnp.concatenate(
        [v_new, jnp.zeros((nb, pad_rows - t, KV_DIM), F32)], axis=1).astype(BF16)

    q_io = lax.broadcasted_iota(jnp.int32, (t, S_KEYS), 0)
    s_io = lax.broadcasted_iota(jnp.int32, (t, S_KEYS), 1)
    dist = q_io + WINDOW - s_io
    valid = (dist >= 0) & (dist < WINDOW) & (s_io + (PAST_LEN - WINDOW) >= 0)
    dist_f = dist.astype(F32)

    qs = _stack_q_heads(z_ref[:, :, Q_OFF:Q_OFF + ATTN_DIM])
    s = jnp.einsum('bqd,bkd->bqk', qs, kx_ref[...], preferred_element_type=F32)
    es, rinv = _softmax_heads(s, valid, dist_f, sinks_ref, t)
    o = jnp.einsum('bqk,bkd->bqd', jnp.concatenate(es, axis=1), vx_ref[...], preferred_element_type=F32)
    o_heads = [o[:, h * t:(h + 1) * t, :] * rinv[h] for h in range(N_HEADS)]
    attn = _unstack_heads(o_heads)
    mix_ref[:, 0:ATTN_DIM] = attn.reshape(nb * t, ATTN_DIM)

    nk_ref[:, 0:WINDOW - t, :] = sk_ref[:, t:, :]
    nk_ref[:, WINDOW - t:, :] = k_new
    nv_ref[:, 0:WINDOW - t, :] = sv_ref[:, t:, :]
    nv_ref[:, WINDOW - t:, :] = v_new

    hist = CONV_K - 1
    gx_ref[:, 0:hist, :] = sc_ref[...]
    gx_ref[:, hist:hist + t, :] = (z_ref[:, :, GA_OFF:GA_OFF + CONV_DIM]
                                   * jax.nn.sigmoid(z_ref[:, :, GB_OFF:GB_OFF + CONV_DIM]))
    c = jnp.broadcast_to(dwb_ref[...], (nb, t, CONV_DIM))
    for j in range(CONV_K):
        c = c + dww_ref[j:j + 1, :] * gx_ref[:, j:j + t, :]
    conv = _conv_tail(c.reshape(nb * t, CONV_DIM), lng_ref, lnb_ref, pw_ref)
    mix_ref[:, CONV_OUT_OFF:CONV_OUT_OFF + CONV_DIM] = conv
    nc_ref[...] = gx_ref[:, t:t + hist, :]

    hist = POOL_MAX - 1
    px_ref[:, 0:hist, :] = sp_ref[...]
    px_ref[:, hist:hist + t, :] = z_ref[:, :, PU_OFF:PU_OFF + POOL_DIM]
    cur = px_ref[:, hist:hist + t, :]
    run = cur
    sums = []
    for back in range(1, POOL_MAX):
        run = run + px_ref[:, hist - back:hist - back + t, :]
        if back + 1 in POOL_WINDOWS:
            sums.append(run)
    pos = PAST_LEN + lax.broadcasted_iota(jnp.int32, (t, 1), 0)
    cnt = jnp.minimum(_pool_window_lane(), pos + 1).astype(F32)
    sums = [x.reshape(nb * t, POOL_DIM) for x in sums]
    cnt = jnp.broadcast_to(cnt, (nb, t, POOL_DIM)).reshape(nb * t, POOL_DIM)
    pool = _pool_tail(sums, cnt, cur.reshape(nb * t, POOL_DIM), poolw_ref, pools_ref)
    mix_ref[:, POOL_OUT_OFF:POOL_OUT_OFF + POOL_DIM] = pool
    np_ref[...] = px_ref[:, t:t + hist, :]


def _sample_mixer(z, sk, sv, sc, sp, sinks, dww, dwb, lng, lnb, pw, poolw, pools):
    nb = SEQ_BLK
    seq3 = lambda j: (j, 0, 0)
    small = lambda shape: pl.BlockSpec(shape, lambda j: (0,) * len(shape))
    return pl.pallas_call(
        _sample_mixer_kernel,
        grid=(DEC_BATCH // nb,),
        in_specs=[pl.BlockSpec(memory_space=pltpu.SMEM),
                  pl.BlockSpec((nb, DEC_SEQ, IN_DIM), seq3),
                  pl.BlockSpec((nb, WINDOW, KV_DIM), seq3), pl.BlockSpec((nb, WINDOW, KV_DIM), seq3),
                  pl.BlockSpec((nb, CONV_K - 1, CONV_DIM), seq3), pl.BlockSpec((nb, POOL_MAX - 1, POOL_DIM), seq3),
                  small((CONV_K, CONV_DIM)), small((1, CONV_DIM)), small((1, CONV_DIM)), small((1, CONV_DIM)),
                  small((CONV_DIM, CONV_DIM)), small((POOL_DIM, POOL_DIM)), small((1, POOL_DIM))],
        out_specs=[pl.BlockSpec((nb * DEC_SEQ, MIX_DIM), lambda j: (j, 0)),
                   pl.BlockSpec((nb, WINDOW, KV_DIM), seq3), pl.BlockSpec((nb, WINDOW, KV_DIM), seq3),
                   pl.BlockSpec((nb, CONV_K - 1, CONV_DIM), seq3),
                   pl.BlockSpec((nb, POOL_MAX - 1, POOL_DIM), seq3)],
        out_shape=[jax.ShapeDtypeStruct((DEC_BATCH * DEC_SEQ, MIX_DIM), F32),
                   jax.ShapeDtypeStruct((DEC_BATCH, WINDOW, KV_DIM), F32),
                   jax.ShapeDtypeStruct((DEC_BATCH, WINDOW, KV_DIM), F32),
                   jax.ShapeDtypeStruct((DEC_BATCH, CONV_K - 1, CONV_DIM), F32),
                   jax.ShapeDtypeStruct((DEC_BATCH, POOL_MAX - 1, POOL_DIM), F32)],
        scratch_shapes=[pltpu.VMEM((nb, S_KEYS, KV_DIM), BF16),
                        pltpu.VMEM((nb, S_KEYS, KV_DIM), BF16),
                        pltpu.VMEM((nb, CONV_K - 1 + DEC_SEQ + 2, CONV_DIM), F32),
                        pltpu.VMEM((nb, POOL_MAX - 1 + DEC_SEQ + 1, POOL_DIM), F32)],
        compiler_params=pltpu.CompilerParams(dimension_semantics=("arbitrary",),
                                             vmem_limit_bytes=VMEM_LIMIT),
        name="sample_mixer",
    )(sinks, z, sk, sv, sc, sp, dww, dwb, lng, lnb, pw, poolw, pools)


def _block_diag(pool_w):
    eye = jnp.eye(N_POOL_GROUPS, dtype=pool_w.dtype)
    return jnp.einsum('gcd,gh->gchd', pool_w, eye).reshape(POOL_DIM, POOL_DIM)


def kernel(x_prompt, x_sample, state_attn_k, state_attn_v, state_conv, state_pool, norm_g, ffn1_wg, ffn1_wu, ffn1_wd, w_in, attn_sinks, conv_dw_w, conv_dw_b, conv_ln_g, conv_ln_b, conv_pw_w, pool_w, pool_scale, w_out, ffn2_wg, ffn2_wu, ffn2_wd):
    hp = x_prompt.reshape(BATCH * SEQ, D_MODEL)
    hs = x_sample.reshape(DEC_BATCH * DEC_SEQ, D_MODEL)
    sk_all = state_attn_k.reshape(DEPTH, DEC_BATCH, WINDOW, KV_DIM)
    sv_all = state_attn_v.reshape(DEPTH, DEC_BATCH, WINDOW, KV_DIM)
    outs = [[] for _ in range(8)]
    for l in range(DEPTH):
        ng = norm_g[l]
        f1 = (ffn1_wg[l].astype(BF16), ffn1_wu[l].astype(BF16), ffn1_wd[l].astype(BF16))
        f2 = (ffn2_wg[l].astype(BF16), ffn2_wu[l].astype(BF16), ffn2_wd[l].astype(BF16))
        win = w_in[l].astype(BF16)
        wout = w_out[l].astype(BF16)
        mixer_w = (attn_sinks[l], conv_dw_w[l], conv_dw_b[l].reshape(1, CONV_DIM),
                   conv_ln_g[l].reshape(1, CONV_DIM), conv_ln_b[l].reshape(1, CONV_DIM),
                   conv_pw_w[l].astype(BF16), _block_diag(pool_w[l]).astype(BF16),
                   pool_scale[l].reshape(1, POOL_DIM))

        h1p, zp = _ffn_in(hp, ng, *f1, win)
        h1s, zs = _ffn_in(hs, ng, *f1, win)
        mixp, k1, v1, c1, p1 = _prompt_mixer(zp, *mixer_w)
        mixs, k2, v2, c2, p2 = _sample_mixer(zs.reshape(DEC_BATCH, DEC_SEQ, IN_DIM), sk_all[l], sv_all[l],
                                             state_conv[l], state_pool[l], *mixer_w)
        hp = _out_ffn(h1p, mixp, ng, wout, *f2)
        hs = _out_ffn(h1s, mixs, ng, wout, *f2)
        for lst, val in zip(outs, (k1, v1, c1, p1, k2, v2, c2, p2)):
            lst.append(val)

    kv_p = (DEPTH, BATCH, WINDOW, N_KV_HEADS, HEAD_DIM)
    kv_s = (DEPTH, DEC_BATCH, WINDOW, N_KV_HEADS, HEAD_DIM)
    return (hp.reshape(BATCH, SEQ, D_MODEL), hs.reshape(DEC_BATCH, DEC_SEQ, D_MODEL),
            jnp.stack(outs[0]).reshape(kv_p), jnp.stack(outs[1]).reshape(kv_p),
            jnp.stack(outs[2]), jnp.stack(outs[3]),
            jnp.stack(outs[4]).reshape(kv_s), jnp.stack(outs[5]).reshape(kv_s),
            jnp.stack(outs[6]), jnp.stack(outs[7]))
```

```python
import functools

import numpy as np
import jax
import jax.numpy as jnp
from jax import lax
from jax.experimental import pallas as pl
from jax.experimental.pallas import tpu as pltpu

D_MODEL = 1024
BATCH = 8
SEQ = 2048
DEPTH = 4
DEC_BATCH = 128
DEC_SEQ = 8
PAST_LEN = 8192

N_HEADS = 8
HEAD_DIM = 64
N_KV_HEADS = 2
GROUP = N_HEADS // N_KV_HEADS
ATTN_DIM = N_HEADS * HEAD_DIM
KV_DIM = N_KV_HEADS * HEAD_DIM
WINDOW = 128
CONV_DIM = D_MODEL // 4
CONV_K = 31
POOL_DIM = D_MODEL // 4
POOL_WINDOWS = (2, 4, 8, 16)
N_POOL_GROUPS = 4
POOL_GROUP_DIM = POOL_DIM // N_POOL_GROUPS
POOL_MAX = 16
IN_DIM = ATTN_DIM + 2 * KV_DIM + 2 * CONV_DIM + POOL_DIM
MIX_DIM = ATTN_DIM + CONV_DIM + POOL_DIM
D_FF = ((8 * D_MODEL // 3 + 127) // 128) * 128
RMS_EPS = 1e-6
LN_EPS = 1e-5
NEG_INF = -1e30

Q_OFF = 0
K_OFF = ATTN_DIM
V_OFF = K_OFF + KV_DIM
GA_OFF = V_OFF + KV_DIM
GB_OFF = GA_OFF + CONV_DIM
PU_OFF = GB_OFF + CONV_DIM
CONV_OUT_OFF = ATTN_DIM
POOL_OUT_OFF = ATTN_DIM + CONV_DIM

LANES = 128
SUBLANES = 8
TM_FFN = 512
TM_MIX = 256
Q_BLK = 128
CONV_ROWS = 64
CONV_HALO = 32
POOL_HALO = 16
SEQ_BLK = 8
S_KEYS = 144
VMEM_LIMIT = 48 * 1024 * 1024

LOG2E = float(np.log2(np.e))
ALIBI_SLOPES2 = [float(s) * LOG2E
                 for s in np.exp2(-8.0 * np.arange(1, N_HEADS + 1, dtype=np.float32) / N_HEADS)]
Q_SCALE2 = HEAD_DIM ** -0.5 * LOG2E

F32 = jnp.float32
BF16 = jnp.bfloat16


def _rms(x, g):
    return x * lax.rsqrt(jnp.mean(x * x, axis=-1, keepdims=True) + RMS_EPS) * g


def _swiglu(xn, wg_ref, wu_ref, wd_ref):
    g = jnp.dot(xn, wg_ref[...], preferred_element_type=F32)
    u = jnp.dot(xn, wu_ref[...], preferred_element_type=F32)
    a = (g * jax.nn.sigmoid(g) * u).astype(BF16)
    return jnp.dot(a, wd_ref[...], preferred_element_type=F32)


def _ffn_in_kernel(x_ref, ng_ref, wg_ref, wu_ref, wd_ref, win_ref, h_ref, z_ref):
    x = x_ref[...]
    xn = _rms(x, ng_ref[0:1, :]).astype(BF16)
    h = x + 0.5 * _rms(_swiglu(xn, wg_ref, wu_ref, wd_ref), ng_ref[1:2, :])
    h_ref[...] = h
    u = _rms(h, ng_ref[2:3, :]).astype(BF16)
    z_ref[...] = jnp.dot(u, win_ref[...], preferred_element_type=F32)


def _out_ffn_kernel(h_ref, mix_ref, ng_ref, wout_ref, wg_ref, wu_ref, wd_ref, y_ref):
    m = jnp.dot(mix_ref[...].astype(BF16), wout_ref[...], preferred_element_type=F32)
    h = h_ref[...] + _rms(m, ng_ref[3:4, :])
    hn = _rms(h, ng_ref[4:5, :]).astype(BF16)
    y_ref[...] = h + 0.5 * _rms(_swiglu(hn, wg_ref, wu_ref, wd_ref), ng_ref[5:6, :])


def _layer_spec(l, shape, single_buffer=False):
    index_map = lambda *_: (l,) + (0,) * len(shape)
    if single_buffer:
        return pl.BlockSpec((None,) + shape, index_map, pipeline_mode=pl.Buffered(1))
    return pl.BlockSpec((None,) + shape, index_map)


def _ffn_in(l, x, ng, wg, wu, wd, win):
    n = x.shape[0]
    row = lambda i: (i, 0)
    return pl.pallas_call(
        _ffn_in_kernel,
        grid=(n // TM_FFN,),
        in_specs=[pl.BlockSpec((TM_FFN, D_MODEL), row),
                  _layer_spec(l, (6, D_MODEL), True),
                  _layer_spec(l, (D_MODEL, D_FF), True), _layer_spec(l, (D_MODEL, D_FF), True),
                  _layer_spec(l, (D_FF, D_MODEL), True), _layer_spec(l, (D_MODEL, IN_DIM), True)],
        out_specs=[pl.BlockSpec((TM_FFN, D_MODEL), row), pl.BlockSpec((TM_FFN, IN_DIM), row)],
        out_shape=[jax.ShapeDtypeStruct((n, D_MODEL), F32), jax.ShapeDtypeStruct((n, IN_DIM), F32)],
        compiler_params=pltpu.CompilerParams(dimension_semantics=("arbitrary",),
                                             vmem_limit_bytes=VMEM_LIMIT),
        name="ffn_in",
    )(x, ng, wg, wu, wd, win)


def _out_ffn(l, h, mix, ng, wout, wg, wu, wd):
    n = h.shape[0]
    row = lambda i: (i, 0)
    return pl.pallas_call(
        _out_ffn_kernel,
        grid=(n // TM_FFN,),
        in_specs=[pl.BlockSpec((TM_FFN, D_MODEL), row), pl.BlockSpec((TM_FFN, MIX_DIM), row),
                  _layer_spec(l, (6, D_MODEL), True), _layer_spec(l, (MIX_DIM, D_MODEL), True),
                  _layer_spec(l, (D_MODEL, D_FF), True), _layer_spec(l, (D_MODEL, D_FF), True),
                  _layer_spec(l, (D_FF, D_MODEL), True)],
        out_specs=pl.BlockSpec((TM_FFN, D_MODEL), row),
        out_shape=jax.ShapeDtypeStruct((n, D_MODEL), F32),
        compiler_params=pltpu.CompilerParams(dimension_semantics=("arbitrary",),
                                             vmem_limit_bytes=VMEM_LIMIT),
        name="out_ffn",
    )(h, mix, ng, wout, wg, wu, wd)


def _stack_q_heads(q):
    lane = lax.broadcasted_iota(jnp.int32, q.shape[:-1] + (LANES,), q.ndim - 1)
    pieces = []
    for h in range(N_HEADS):
        kvh = h // GROUP
        slab = q[..., (h // 2) * LANES:(h // 2 + 1) * LANES]
        if h % 2 != kvh:
            slab = pltpu.roll(slab, HEAD_DIM, q.ndim - 1)
        keep = (lane >= kvh * HEAD_DIM) & (lane < (kvh + 1) * HEAD_DIM)
        pieces.append((jnp.where(keep, slab, 0.0) * Q_SCALE2).astype(BF16))
    return jnp.concatenate(pieces, axis=-2)


def _unstack_heads(o_heads):
    ndim = o_heads[0].ndim
    lane = lax.broadcasted_iota(jnp.int32, o_heads[0].shape, ndim - 1)
    slabs = []
    for j in range(N_HEADS // 2):
        a, b = o_heads[2 * j], o_heads[2 * j + 1]
        kvh = (2 * j) // GROUP
        if kvh == 0:
            b = pltpu.roll(b, HEAD_DIM, ndim - 1)
        else:
            a = pltpu.roll(a, HEAD_DIM, ndim - 1)
        slabs.append(jnp.where(lane < HEAD_DIM, a, b))
    return jnp.concatenate(slabs, axis=-1)


def _alibi_bias(dist_f):
    return [ALIBI_SLOPES2[h] * dist_f for h in range(N_HEADS)]


def _softmax_heads(s, valid, bias, sinks, rows):
    es, rinv = [], []
    for h in range(N_HEADS):
        sh = s[..., h * rows:(h + 1) * rows, :]
        l = jnp.where(valid, sh - bias[h], NEG_INF)
        m = jnp.maximum(jnp.max(l, axis=-1, keepdims=True), sinks[h])
        e = jnp.exp2(l - m)
        den = jnp.sum(e, axis=-1, keepdims=True) + jnp.exp2(sinks[h] - m)
        es.append(e.astype(BF16))
        rinv.append(1.0 / den)
    return es, rinv


def _conv_tail(c, lng_ref, lnb_ref, pw_ref):
    mu = jnp.mean(c, axis=-1, keepdims=True)
    xc = c - mu
    y = xc * lax.rsqrt(jnp.mean(xc * xc, axis=-1, keepdims=True) + LN_EPS) * lng_ref[...] + lnb_ref[...]
    y = (y * jax.nn.sigmoid(y)).astype(BF16)
    return jnp.dot(y, pw_ref[...], preferred_element_type=F32)


def _pool_tail(sums, cnt, cur, poolw_ref, pools_ref):
    lane = lax.broadcasted_iota(jnp.int32, cur.shape, cur.ndim - 1)
    wsum = sums[-1]
    for g in range(N_POOL_GROUPS - 2, -1, -1):
        wsum = jnp.where(lane < (g + 1) * POOL_GROUP_DIM, sums[g], wsum)
    d = (wsum / cnt - cur).astype(BF16)
    return jnp.dot(d, poolw_ref[...], preferred_element_type=F32) * pools_ref[...]


def _pool_window_lane():
    lane = lax.broadcasted_iota(jnp.int32, (1, POOL_DIM), 1)
    w = jnp.full((1, POOL_DIM), POOL_WINDOWS[-1], jnp.int32)
    for g in range(N_POOL_GROUPS - 2, -1, -1):
        w = jnp.where(lane < (g + 1) * POOL_GROUP_DIM, POOL_WINDOWS[g], w)
    return w


def _dw_conv_rows(gx_ref, dww_ref, r0, rows):
    first = CONV_HALO - (CONV_K - 1)
    out = None
    for r in range(SUBLANES):
        ext = rows if r == 0 else rows + SUBLANES
        acc = None
        for m in range((first + CONV_K - 1) // SUBLANES + 1):
            j = SUBLANES * m + r - first
            if j < 0 or j >= CONV_K:
                continue
            lo = r0 + SUBLANES * m
            term = dww_ref[j:j + 1, :] * gx_ref[lo:lo + ext, :]
            acc = term if acc is None else acc + term
        part = acc if r == 0 else acc[r:r + rows, :]
        out = part if out is None else out + part
    return out


def _prompt_mixer_kernel(sinks_ref, z_ref, dww_ref, dwb_ref, lng_ref, lnb_ref, pw_ref, poolw_ref, pools_ref,
                         mix_ref, nk_ref, nv_ref, nc_ref, np_ref,
                         kx_ref, vx_ref, gx_ref, px_ref, *, layer):
    i = pl.program_id(1)
    tm = TM_MIX

    @pl.when(i == 0)
    def _():
        kx_ref[0:WINDOW, :] = jnp.zeros((WINDOW, KV_DIM), BF16)
        vx_ref[0:WINDOW, :] = jnp.zeros((WINDOW, KV_DIM), BF16)
        gx_ref[0:CONV_HALO, :] = jnp.zeros((CONV_HALO, CONV_DIM), F32)
        px_ref[0:POOL_HALO, :] = jnp.zeros((POOL_HALO, POOL_DIM), F32)

    kx_ref[WINDOW:, :] = z_ref[:, K_OFF:K_OFF + KV_DIM].astype(BF16)
    vx_ref[WINDOW:, :] = z_ref[:, V_OFF:V_OFF + KV_DIM].astype(BF16)
    gx_ref[CONV_HALO:, :] = z_ref[:, GA_OFF:GA_OFF + CONV_DIM] * jax.nn.sigmoid(z_ref[:, GB_OFF:GB_OFF + CONV_DIM])
    px_ref[POOL_HALO:, :] = z_ref[:, PU_OFF:PU_OFF + POOL_DIM]

    q_io = lax.broadcasted_iota(jnp.int32, (Q_BLK, WINDOW + Q_BLK), 0)
    s_io = lax.broadcasted_iota(jnp.int32, (Q_BLK, WINDOW + Q_BLK), 1)
    dist = q_io + WINDOW - s_io
    band = (dist >= 0) & (dist < WINDOW)
    bias = _alibi_bias(dist.astype(F32))
    sinks = [sinks_ref[layer, h] * LOG2E for h in range(N_HEADS)]
    for blk in range(tm // Q_BLK):
        r0 = blk * Q_BLK
        first_key_pos = i * tm + r0 - WINDOW
        valid = band & (s_io + first_key_pos >= 0)
        qs = _stack_q_heads(z_ref[r0:r0 + Q_BLK, Q_OFF:Q_OFF + ATTN_DIM])
        kb = kx_ref[r0:r0 + WINDOW + Q_BLK, :]
        vb = vx_ref[r0:r0 + WINDOW + Q_BLK, :]
        s = lax.dot_general(qs, kb, (((1,), (1,)), ((), ())), preferred_element_type=F32)
        es, rinv = _softmax_heads(s, valid, bias, sinks, Q_BLK)
        o = jnp.dot(jnp.concatenate(es, axis=0), vb, preferred_element_type=F32)
        o_heads = [o[h * Q_BLK:(h + 1) * Q_BLK, :] * rinv[h] for h in range(N_HEADS)]
        mix_ref[r0:r0 + Q_BLK, 0:ATTN_DIM] = _unstack_heads(o_heads).astype(mix_ref.dtype)

    for rc in range(tm // CONV_ROWS):
        r0 = rc * CONV_ROWS
        c = _dw_conv_rows(gx_ref, dww_ref, r0, CONV_ROWS) + dwb_ref[...]
        mix_ref[r0:r0 + CONV_ROWS, CONV_OUT_OFF:CONV_OUT_OFF + CONV_DIM] = (
            _conv_tail(c, lng_ref, lnb_ref, pw_ref).astype(mix_ref.dtype))

    w_lane = _pool_window_lane()
    for rc in range(tm // Q_BLK):
        r0 = rc * Q_BLK
        run = px_ref[r0:r0 + POOL_HALO + Q_BLK, :]
        cur = run[POOL_HALO:, :]
        sums = []
        span = 1
        while span < POOL_MAX:
            run = run + pltpu.roll(run, span, 0)
            span *= 2
            if span in POOL_WINDOWS:
                sums.append(run[POOL_HALO:, :])
        pos = i * tm + r0 + lax.broadcasted_iota(jnp.int32, (Q_BLK, 1), 0)
        cnt = jnp.minimum(w_lane, pos + 1).astype(F32)
        mix_ref[r0:r0 + Q_BLK, POOL_OUT_OFF:POOL_OUT_OFF + POOL_DIM] = (
            _pool_tail(sums, cnt, cur, poolw_ref, pools_ref).astype(mix_ref.dtype))

    nk_ref[...] = z_ref[tm - WINDOW:, K_OFF:K_OFF + KV_DIM]
    nv_ref[...] = z_ref[tm - WINDOW:, V_OFF:V_OFF + KV_DIM]
    nc_ref[...] = gx_ref[CONV_HALO + tm - (CONV_K - 1):, :]
    np_ref[...] = px_ref[POOL_HALO + tm - (POOL_MAX - 1):, :]

    kx_ref[0:WINDOW, :] = kx_ref[tm:tm + WINDOW, :]
    vx_ref[0:WINDOW, :] = vx_ref[tm:tm + WINDOW, :]
    gx_ref[0:CONV_HALO, :] = gx_ref[tm:tm + CONV_HALO, :]
    px_ref[0:POOL_HALO, :] = px_ref[tm:tm + POOL_HALO, :]


def _mixer_weight_specs(l):
    return [pl.BlockSpec(memory_space=pltpu.SMEM),
            _layer_spec(l, (CONV_K, CONV_DIM)), _layer_spec(l, (1, CONV_DIM)),
            _layer_spec(l, (1, CONV_DIM)), _layer_spec(l, (1, CONV_DIM)),
            _layer_spec(l, (CONV_DIM, CONV_DIM)), _layer_spec(l, (POOL_DIM, POOL_DIM)),
            _layer_spec(l, (1, POOL_DIM))]


def _prompt_mixer(l, z, sinks, dww, dwb, lng, lnb, pw, poolw, pools):
    tiles = SEQ // TM_MIX
    tok = lambda b, i: (b * tiles + i, 0)
    seq3 = lambda b, i: (b, 0, 0)
    wspecs = _mixer_weight_specs(l)
    return pl.pallas_call(
        functools.partial(_prompt_mixer_kernel, layer=l),
        grid=(BATCH, tiles),
        in_specs=[wspecs[0], pl.BlockSpec((TM_MIX, IN_DIM), tok)] + wspecs[1:],
        out_specs=[pl.BlockSpec((TM_MIX, MIX_DIM), tok),
                   pl.BlockSpec((None, WINDOW, KV_DIM), seq3), pl.BlockSpec((None, WINDOW, KV_DIM), seq3),
                   pl.BlockSpec((None, CONV_K - 1, CONV_DIM), seq3),
                   pl.BlockSpec((None, POOL_MAX - 1, POOL_DIM), seq3)],
        out_shape=[jax.ShapeDtypeStruct((BATCH * SEQ, MIX_DIM), BF16),
                   jax.ShapeDtypeStruct((BATCH, WINDOW, KV_DIM), F32),
                   jax.ShapeDtypeStruct((BATCH, WINDOW, KV_DIM), F32),
                   jax.ShapeDtypeStruct((BATCH, CONV_K - 1, CONV_DIM), F32),
                   jax.ShapeDtypeStruct((BATCH, POOL_MAX - 1, POOL_DIM), F32)],
        scratch_shapes=[pltpu.VMEM((WINDOW + TM_MIX, KV_DIM), BF16),
                        pltpu.VMEM((WINDOW + TM_MIX, KV_DIM), BF16),
                        pltpu.VMEM((CONV_HALO + TM_MIX, CONV_DIM), F32),
                        pltpu.VMEM((POOL_HALO + TM_MIX, POOL_DIM), F32)],
        compiler_params=pltpu.CompilerParams(dimension_semantics=("arbitrary", "arbitrary"),
                                             vmem_limit_bytes=VMEM_LIMIT),
        name="prompt_mixer",
    )(sinks, z, dww, dwb, lng, lnb, pw, poolw, pools)


def _sample_mixer_kernel(sinks_ref, z_ref, sk_ref, sv_ref, sc_ref, sp_ref,
                         dww_ref, dwb_ref, lng_ref, lnb_ref, pw_ref, poolw_ref, pools_ref,
                         mix_ref, nk_ref, nv_ref, nc_ref, np_ref,
                         kx_ref, vx_ref, gx_ref, px_ref, *, layer):
    nb, t = SEQ_BLK, DEC_SEQ
    k_new = z_ref[:, :, K_OFF:K_OFF + KV_DIM]
    v_new = z_ref[:, :, V_OFF:V_OFF + KV_DIM]

    kx_ref[:, 0:WINDOW, :] = sk_ref[...].astype(BF16)
    vx_ref[:, 0:WINDOW, :] = sv_ref[...].astype(BF16)
    pad_rows = S_KEYS - WINDOW
    kx_ref[:, WINDOW:, :] = jnp.concatenate(
        [k_new, jnp.zeros((nb, pad_rows - t, KV_DIM), F32)], axis=1).astype(BF16)
    vx_ref[:, WINDOW:, :] = jnp.concatenate(
        [v_new, jnp.zeros((nb, pad_rows - t, KV_DIM), F32)], axis=1).astype(BF16)

    q_io = lax.broadcasted_iota(jnp.int32, (t, S_KEYS), 0)
    s_io = lax.broadcasted_iota(jnp.int32, (t, S_KEYS), 1)
    dist = q_io + WINDOW - s_io
    valid = (dist >= 0) & (dist < WINDOW) & (s_io + (PAST_LEN - WINDOW) >= 0)
    bias = _alibi_bias(dist.astype(F32))
    sinks = [sinks_ref[layer, h] * LOG2E for h in range(N_HEADS)]

    qs = _stack_q_heads(z_ref[:, :, Q_OFF:Q_OFF + ATTN_DIM])
    s = jnp.einsum('bqd,bkd->bqk', qs, kx_ref[...], preferred_element_type=F32)
    es, rinv = _softmax_heads(s, valid, bias, sinks, t)
    o = jnp.einsum('bqk,bkd->bqd', jnp.concatenate(es, axis=1), vx_ref[...], preferred_element_type=F32)
    o_heads = [o[:, h * t:(h + 1) * t, :] * rinv[h] for h in range(N_HEADS)]
    attn = _unstack_heads(o_heads)
    mix_ref[:, 0:ATTN_DIM] = attn.reshape(nb * t, ATTN_DIM)

    nk_ref[:, 0:WINDOW - t, :] = sk_ref[:, t:, :]
    nk_ref[:, WINDOW - t:, :] = k_new
    nv_ref[:, 0:WINDOW - t, :] = sv_ref[:, t:, :]
    nv_ref[:, WINDOW - t:, :] = v_new

    hist = CONV_K - 1
    gx_ref[:, 0:hist, :] = sc_ref[...]
    gx_ref[:, hist:hist + t, :] = (z_ref[:, :, GA_OFF:GA_OFF + CONV_DIM]
                                   * jax.nn.sigmoid(z_ref[:, :, GB_OFF:GB_OFF + CONV_DIM]))
    c = jnp.broadcast_to(dwb_ref[...], (nb, t, CONV_DIM))
    for j in range(CONV_K):
        c = c + dww_ref[j:j + 1, :] * gx_ref[:, j:j + t, :]
    conv = _conv_tail(c.reshape(nb * t, CONV_DIM), lng_ref, lnb_ref, pw_ref)
    mix_ref[:, CONV_OUT_OFF:CONV_OUT_OFF + CONV_DIM] = conv
    nc_ref[...] = gx_ref[:, t:t + hist, :]

    hist = POOL_MAX - 1
    px_ref[:, 0:hist, :] = sp_ref[...]
    px_ref[:, hist:hist + t, :] = z_ref[:, :, PU_OFF:PU_OFF + POOL_DIM]
    cur = px_ref[:, hist:hist + t, :]
    run = cur
    sums = []
    for back in range(1, POOL_MAX):
        run = run + px_ref[:, hist - back:hist - back + t, :]
        if back + 1 in POOL_WINDOWS:
            sums.append(run)
    pos = PAST_LEN + lax.broadcasted_iota(jnp.int32, (t, 1), 0)
    cnt = jnp.minimum(_pool_window_lane(), pos + 1).astype(F32)
    sums = [x.reshape(nb * t, POOL_DIM) for x in sums]
    cnt = jnp.broadcast_to(cnt, (nb, t, POOL_DIM)).reshape(nb * t, POOL_DIM)
    pool = _pool_tail(sums, cnt, cur.reshape(nb * t, POOL_DIM), poolw_ref, pools_ref)
    mix_ref[:, POOL_OUT_OFF:POOL_OUT_OFF + POOL_DIM] = pool
    np_ref[...] = px_ref[:, t:t + hist, :]


def _sample_mixer(l, z, sk, sv, sc, sp, sinks, dww, dwb, lng, lnb, pw, poolw, pools):
    nb = SEQ_BLK
    seq3 = lambda j: (j, 0, 0)
    lseq = lambda j: (l, j, 0, 0)
    wspecs = _mixer_weight_specs(l)
    return pl.pallas_call(
        functools.partial(_sample_mixer_kernel, layer=l),
        grid=(DEC_BATCH // nb,),
        in_specs=[wspecs[0],
                  pl.BlockSpec((nb, DEC_SEQ, IN_DIM), seq3),
                  pl.BlockSpec((None, nb, WINDOW, KV_DIM), lseq), pl.BlockSpec((None, nb, WINDOW, KV_DIM), lseq),
                  pl.BlockSpec((None, nb, CONV_K - 1, CONV_DIM), lseq),
                  pl.BlockSpec((None, nb, POOL_MAX - 1, POOL_DIM), lseq)] + wspecs[1:],
        out_specs=[pl.BlockSpec((nb * DEC_SEQ, MIX_DIM), lambda j: (j, 0)),
                   pl.BlockSpec((nb, WINDOW, KV_DIM), seq3), pl.BlockSpec((nb, WINDOW, KV_DIM), seq3),
                   pl.BlockSpec((nb, CONV_K - 1, CONV_DIM), seq3),
                   pl.BlockSpec((nb, POOL_MAX - 1, POOL_DIM), seq3)],
        out_shape=[jax.ShapeDtypeStruct((DEC_BATCH * DEC_SEQ, MIX_DIM), F32),
                   jax.ShapeDtypeStruct((DEC_BATCH, WINDOW, KV_DIM), F32),
                   jax.ShapeDtypeStruct((DEC_BATCH, WINDOW, KV_DIM), F32),
                   jax.ShapeDtypeStruct((DEC_BATCH, CONV_K - 1, CONV_DIM), F32),
                   jax.ShapeDtypeStruct((DEC_BATCH, POOL_MAX - 1, POOL_DIM), F32)],
        scratch_shapes=[pltpu.VMEM((nb, S_KEYS, KV_DIM), BF16),
                        pltpu.VMEM((nb, S_KEYS, KV_DIM), BF16),
                        pltpu.VMEM((nb, CONV_K - 1 + DEC_SEQ + 2, CONV_DIM), F32),
                        pltpu.VMEM((nb, POOL_MAX - 1 + DEC_SEQ + 1, POOL_DIM), F32)],
        compiler_params=pltpu.CompilerParams(dimension_semantics=("arbitrary",),
                                             vmem_limit_bytes=VMEM_LIMIT),
        name="sample_mixer",
    )(sinks, z, sk, sv, sc, sp, dww, dwb, lng, lnb, pw, poolw, pools)


def _block_diag(pool_w):
    eye = jnp.eye(N_POOL_GROUPS, dtype=pool_w.dtype)
    return jnp.einsum('lgcd,gh->lgchd', pool_w, eye).reshape(DEPTH, POOL_DIM, POOL_DIM)


def kernel(x_prompt, x_sample, state_attn_k, state_attn_v, state_conv, state_pool, norm_g, ffn1_wg, ffn1_wu, ffn1_wd, w_in, attn_sinks, conv_dw_w, conv_dw_b, conv_ln_g, conv_ln_b, conv_pw_w, pool_w, pool_scale, w_out, ffn2_wg, ffn2_wu, ffn2_wd):
    hp = x_prompt.reshape(BATCH * SEQ, D_MODEL)
    hs = x_sample.reshape(DEC_BATCH * DEC_SEQ, D_MODEL)
    sk_all = state_attn_k.reshape(DEPTH, DEC_BATCH, WINDOW, KV_DIM)
    sv_all = state_attn_v.reshape(DEPTH, DEC_BATCH, WINDOW, KV_DIM)
    f1 = (ffn1_wg.astype(BF16), ffn1_wu.astype(BF16), ffn1_wd.astype(BF16))
    f2 = (ffn2_wg.astype(BF16), ffn2_wu.astype(BF16), ffn2_wd.astype(BF16))
    win = w_in.astype(BF16)
    wout = w_out.astype(BF16)
    mixer_w = (attn_sinks, conv_dw_w, conv_dw_b.reshape(DEPTH, 1, CONV_DIM),
               conv_ln_g.reshape(DEPTH, 1, CONV_DIM), conv_ln_b.reshape(DEPTH, 1, CONV_DIM),
               conv_pw_w.astype(BF16), _block_diag(pool_w).astype(BF16),
               pool_scale.reshape(DEPTH, 1, POOL_DIM))
    outs = [[] for _ in range(8)]
    for l in range(DEPTH):
        h1p, zp = _ffn_in(l, hp, norm_g, *f1, win)
        h1s, zs = _ffn_in(l, hs, norm_g, *f1, win)
        mixp, k1, v1, c1, p1 = _prompt_mixer(l, zp, *mixer_w)
        mixs, k2, v2, c2, p2 = _sample_mixer(l, zs.reshape(DEC_BATCH, DEC_SEQ, IN_DIM), sk_all, sv_all,
                                             state_conv, state_pool, *mixer_w)
        hp = _out_ffn(l, h1p, mixp, norm_g, wout, *f2)
        hs = _out_ffn(l, h1s, mixs, norm_g, wout, *f2)
        for lst, val in zip(outs, (k1, v1, c1, p1, k2, v2, c2, p2)):
            lst.append(val)

    kv_p = (DEPTH, BATCH, WINDOW, N_KV_HEADS, HEAD_DIM)
    kv_s = (DEPTH, DEC_BATCH, WINDOW, N_KV_HEADS, HEAD_DIM)
    return (hp.reshape(BATCH, SEQ, D_MODEL), hs.reshape(DEC_BATCH, DEC_SEQ, D_MODEL),
            jnp.stack(outs[0]).reshape(kv_p), jnp.stack(outs[1]).reshape(kv_p),
            jnp.stack(outs[2]), jnp.stack(outs[3]),
            jnp.stack(outs[4]).reshape(kv_s), jnp.stack(outs[5]).reshape(kv_s),
            jnp.stack(outs[6]), jnp.stack(outs[7]))
```

```python
import functools

import numpy as np
import jax
import jax.numpy as jnp
from jax import lax
from jax.experimental import pallas as pl
from jax.experimental.pallas import tpu as pltpu

D_MODEL = 1024
BATCH = 8
SEQ = 2048
DEPTH = 4
DEC_BATCH = 128
DEC_SEQ = 8
PAST_LEN = 8192

N_HEADS = 8
HEAD_DIM = 64
N_KV_HEADS = 2
GROUP = N_HEADS // N_KV_HEADS
ATTN_DIM = N_HEADS * HEAD_DIM
KV_DIM = N_KV_HEADS * HEAD_DIM
WINDOW = 128
CONV_DIM = D_MODEL // 4
CONV_K = 31
POOL_DIM = D_MODEL // 4
POOL_WINDOWS = (2, 4, 8, 16)
N_POOL_GROUPS = 4
POOL_GROUP_DIM = POOL_DIM // N_POOL_GROUPS
POOL_MAX = 16
IN_DIM = ATTN_DIM + 2 * KV_DIM + 2 * CONV_DIM + POOL_DIM
MIX_DIM = ATTN_DIM + CONV_DIM + POOL_DIM
D_FF = ((8 * D_MODEL // 3 + 127) // 128) * 128
RMS_EPS = 1e-6
LN_EPS = 1e-5
NEG_INF = -1e30

Q_OFF = 0
K_OFF = ATTN_DIM
V_OFF = K_OFF + KV_DIM
GA_OFF = V_OFF + KV_DIM
GB_OFF = GA_OFF + CONV_DIM
PU_OFF = GB_OFF + CONV_DIM
CONV_OUT_OFF = ATTN_DIM
POOL_OUT_OFF = ATTN_DIM + CONV_DIM

LANES = 128
SUBLANES = 8
TM_FFN = 512
MXU_TILE = 256
FF_SPLITS_FUSED = (2 * MXU_TILE, 2 * MXU_TILE, 2 * MXU_TILE, 2 * MXU_TILE, 3 * MXU_TILE)
TM_MIX = 512
Q_BLK = 128
CONV_ROWS = 64
CONV_HALO = 32
POOL_HALO = 16
SEQ_BLK = 8
S_KEYS = 144
VMEM_LIMIT = 48 * 1024 * 1024
VMEM_LIMIT_FUSED = 54 * 1024 * 1024

LOG2E = float(np.log2(np.e))
ALIBI_SLOPES2 = [float(s) * LOG2E
                 for s in np.exp2(-8.0 * np.arange(1, N_HEADS + 1, dtype=np.float32) / N_HEADS)]
Q_SCALE2 = HEAD_DIM ** -0.5 * LOG2E

F32 = jnp.float32
BF16 = jnp.bfloat16


def _rms(x, g):
    return x * lax.rsqrt(jnp.mean(x * x, axis=-1, keepdims=True) + RMS_EPS) * g


def _swiglu(xn, wg_ref, wu_ref, wd_ref, splits=(D_FF,), before=(), after=()):
    acc = None
    lo = 0
    for c, width in enumerate(splits):
        if c < len(before) and before[c] is not None:
            before[c]()
        sl = slice(lo, lo + width)
        lo += width
        g = jnp.dot(xn, wg_ref[:, sl], preferred_element_type=F32)
        u = jnp.dot(xn, wu_ref[:, sl], preferred_element_type=F32)
        a = (g * jax.nn.sigmoid(g) * u).astype(BF16)
        part = jnp.dot(a, wd_ref[sl, :], preferred_element_type=F32)
        acc = part if acc is None else acc + part
        if c < len(after) and after[c] is not None:
            after[c]()
    return acc


def _ffn_in_kernel(x_ref, ng_ref, wg_ref, wu_ref, wd_ref, win_ref, h_ref, z_ref):
    x = x_ref[...]
    xn = _rms(x, ng_ref[0:1, :]).astype(BF16)
    h = x + 0.5 * _rms(_swiglu(xn, wg_ref, wu_ref, wd_ref), ng_ref[1:2, :])
    h_ref[...] = h
    u = _rms(h, ng_ref[2:3, :]).astype(BF16)
    z_ref[...] = jnp.dot(u, win_ref[...], preferred_element_type=F32)


def _out_ffn_kernel(h_ref, mix_ref, ng_ref, wout_ref, wg_ref, wu_ref, wd_ref, y_ref,
                    splits=(D_FF,), before=(), after=()):
    m = jnp.dot(mix_ref[...].astype(BF16), wout_ref[...], preferred_element_type=F32)
    h = h_ref[...] + _rms(m, ng_ref[3:4, :])
    hn = _rms(h, ng_ref[4:5, :]).astype(BF16)
    y_ref[...] = h + 0.5 * _rms(_swiglu(hn, wg_ref, wu_ref, wd_ref, splits, before, after), ng_ref[5:6, :])


def _layer_spec(l, shape, single_buffer=False):
    index_map = lambda *_: (l,) + (0,) * len(shape)
    if single_buffer:
        return pl.BlockSpec((None,) + shape, index_map, pipeline_mode=pl.Buffered(1))
    return pl.BlockSpec((None,) + shape, index_map)


def _ffn_in(l, x, ng, wg, wu, wd, win):
    n = x.shape[0]
    row = lambda i: (i, 0)
    return pl.pallas_call(
        _ffn_in_kernel,
        grid=(n // TM_FFN,),
        in_specs=[pl.BlockSpec((TM_FFN, D_MODEL), row),
                  _layer_spec(l, (6, D_MODEL), True),
                  _layer_spec(l, (D_MODEL, D_FF), True), _layer_spec(l, (D_MODEL, D_FF), True),
                  _layer_spec(l, (D_FF, D_MODEL), True), _layer_spec(l, (D_MODEL, IN_DIM), True)],
        out_specs=[pl.BlockSpec((TM_FFN, D_MODEL), row), pl.BlockSpec((TM_FFN, IN_DIM), row)],
        out_shape=[jax.ShapeDtypeStruct((n, D_MODEL), F32), jax.ShapeDtypeStruct((n, IN_DIM), F32)],
        compiler_params=pltpu.CompilerParams(dimension_semantics=("arbitrary",),
                                             vmem_limit_bytes=VMEM_LIMIT),
        name="ffn_in",
    )(x, ng, wg, wu, wd, win)


def _out_ffn(l, h, mix, ng, wout, wg, wu, wd):
    n = h.shape[0]
    row = lambda i: (i, 0)
    return pl.pallas_call(
        _out_ffn_kernel,
        grid=(n // TM_FFN,),
        in_specs=[pl.BlockSpec((TM_FFN, D_MODEL), row), pl.BlockSpec((TM_FFN, MIX_DIM), row),
                  _layer_spec(l, (6, D_MODEL), True), _layer_spec(l, (MIX_DIM, D_MODEL), True),
                  _layer_spec(l, (D_MODEL, D_FF), True), _layer_spec(l, (D_MODEL, D_FF), True),
                  _layer_spec(l, (D_FF, D_MODEL), True)],
        out_specs=pl.BlockSpec((TM_FFN, D_MODEL), row),
        out_shape=jax.ShapeDtypeStruct((n, D_MODEL), F32),
        compiler_params=pltpu.CompilerParams(dimension_semantics=("arbitrary",),
                                             vmem_limit_bytes=VMEM_LIMIT),
        name="out_ffn",
    )(h, mix, ng, wout, wg, wu, wd)


def _stack_q_heads(q):
    lane = lax.broadcasted_iota(jnp.int32, q.shape[:-1] + (LANES,), q.ndim - 1)
    pieces = []
    for h in range(N_HEADS):
        kvh = h // GROUP
        slab = q[..., (h // 2) * LANES:(h // 2 + 1) * LANES]
        if h % 2 != kvh:
            slab = pltpu.roll(slab, HEAD_DIM, q.ndim - 1)
        keep = (lane >= kvh * HEAD_DIM) & (lane < (kvh + 1) * HEAD_DIM)
        pieces.append((jnp.where(keep, slab, 0.0) * Q_SCALE2).astype(BF16))
    return jnp.concatenate(pieces, axis=-2)


def _unstack_heads(o_heads):
    ndim = o_heads[0].ndim
    lane = lax.broadcasted_iota(jnp.int32, o_heads[0].shape, ndim - 1)
    slabs = []
    for j in range(N_HEADS // 2):
        a, b = o_heads[2 * j], o_heads[2 * j + 1]
        kvh = (2 * j) // GROUP
        if kvh == 0:
            b = pltpu.roll(b, HEAD_DIM, ndim - 1)
        else:
            a = pltpu.roll(a, HEAD_DIM, ndim - 1)
        slabs.append(jnp.where(lane < HEAD_DIM, a, b))
    return jnp.concatenate(slabs, axis=-1)


def _alibi_bias(dist_f):
    return [ALIBI_SLOPES2[h] * dist_f for h in range(N_HEADS)]


def _softmax_heads(s, valid, bias, sinks, rows):
    es, rinv = [], []
    for h in range(N_HEADS):
        sh = s[..., h * rows:(h + 1) * rows, :]
        l = jnp.where(valid, sh - bias[h], NEG_INF)
        m = jnp.maximum(jnp.max(l, axis=-1, keepdims=True), sinks[h])
        e = jnp.exp2(l - m)
        den = jnp.sum(e, axis=-1, keepdims=True) + jnp.exp2(sinks[h] - m)
        es.append(e.astype(BF16))
        rinv.append(1.0 / den)
    return es, rinv


def _conv_ln_act(c, lng_ref, lnb_ref):
    mu = jnp.mean(c, axis=-1, keepdims=True)
    xc = c - mu
    y = xc * lax.rsqrt(jnp.mean(xc * xc, axis=-1, keepdims=True) + LN_EPS) * lng_ref[...] + lnb_ref[...]
    return (y * jax.nn.sigmoid(y)).astype(BF16)


def _conv_tail(c, lng_ref, lnb_ref, pw_ref):
    return jnp.dot(_conv_ln_act(c, lng_ref, lnb_ref), pw_ref[...], preferred_element_type=F32)


def _pool_diff(sums, cnt, cur):
    lane = lax.broadcasted_iota(jnp.int32, cur.shape, cur.ndim - 1)
    wsum = sums[-1]
    for g in range(N_POOL_GROUPS - 2, -1, -1):
        wsum = jnp.where(lane < (g + 1) * POOL_GROUP_DIM, sums[g], wsum)
    return (wsum / cnt - cur).astype(BF16)


def _pool_tail(sums, cnt, cur, poolw_ref, pools_ref):
    return jnp.dot(_pool_diff(sums, cnt, cur), poolw_ref[...], preferred_element_type=F32) * pools_ref[...]


def _pool_window_lane():
    lane = lax.broadcasted_iota(jnp.int32, (1, POOL_DIM), 1)
    w = jnp.full((1, POOL_DIM), POOL_WINDOWS[-1], jnp.int32)
    for g in range(N_POOL_GROUPS - 2, -1, -1):
        w = jnp.where(lane < (g + 1) * POOL_GROUP_DIM, POOL_WINDOWS[g], w)
    return w


def _dw_conv_rows(gx_ref, dww_ref, r0, rows):
    first = CONV_HALO - (CONV_K - 1)
    out = None
    for r in range(SUBLANES):
        ext = rows if r == 0 else rows + SUBLANES
        acc = None
        for m in range((first + CONV_K - 1) // SUBLANES + 1):
            j = SUBLANES * m + r - first
            if j < 0 or j >= CONV_K:
                continue
            lo = r0 + SUBLANES * m
            term = dww_ref[j:j + 1, :] * gx_ref[lo:lo + ext, :]
            acc = term if acc is None else acc + term
        part = acc if r == 0 else acc[r:r + rows, :]
        out = part if out is None else out + part
    return out


def _prompt_mixer_pieces(i, sinks_ref, z_ref, dww_ref, dwb_ref, lng_ref, lnb_ref, pw_ref, poolw_ref, pools_ref,
                         mix_ref, nk_ref, nv_ref, nc_ref, np_ref,
                         kx_ref, vx_ref, gx_ref, px_ref, layer):
    tm = TM_MIX

    kx_ref[WINDOW:, :] = z_ref[:, K_OFF:K_OFF + KV_DIM].astype(BF16)
    vx_ref[WINDOW:, :] = z_ref[:, V_OFF:V_OFF + KV_DIM].astype(BF16)
    gx_ref[CONV_HALO:, :] = z_ref[:, GA_OFF:GA_OFF + CONV_DIM] * jax.nn.sigmoid(z_ref[:, GB_OFF:GB_OFF + CONV_DIM])
    px_ref[POOL_HALO:, :] = z_ref[:, PU_OFF:PU_OFF + POOL_DIM]

    q_io = lax.broadcasted_iota(jnp.int32, (Q_BLK, WINDOW + Q_BLK), 0)
    s_io = lax.broadcasted_iota(jnp.int32, (Q_BLK, WINDOW + Q_BLK), 1)
    dist = q_io + WINDOW - s_io
    band = (dist >= 0) & (dist < WINDOW)
    bias = _alibi_bias(dist.astype(F32))
    sinks = [sinks_ref[layer, h] * LOG2E for h in range(N_HEADS)]
    w_lane = _pool_window_lane()

    def head(blk):
        r0 = blk * Q_BLK
        first_key_pos = i * tm + r0 - WINDOW
        valid = band & (s_io + first_key_pos >= 0)
        qs = _stack_q_heads(z_ref[r0:r0 + Q_BLK, Q_OFF:Q_OFF + ATTN_DIM])
        kb = kx_ref[r0:r0 + WINDOW + Q_BLK, :]
        s = lax.dot_general(qs, kb, (((1,), (1,)), ((), ())), preferred_element_type=F32)
        es, rinv = _softmax_heads(s, valid, bias, sinks, Q_BLK)
        ys = []
        for rc in range(Q_BLK // CONV_ROWS):
            c = _dw_conv_rows(gx_ref, dww_ref, r0 + rc * CONV_ROWS, CONV_ROWS) + dwb_ref[...]
            ys.append(_conv_ln_act(c, lng_ref, lnb_ref))
        run = px_ref[r0:r0 + POOL_HALO + Q_BLK, :]
        cur = run[POOL_HALO:, :]
        sums = []
        span = 1
        while span < POOL_MAX:
            run = run + pltpu.roll(run, span, 0)
            span *= 2
            if span in POOL_WINDOWS:
                sums.append(run[POOL_HALO:, :])
        pos = i * tm + r0 + lax.broadcasted_iota(jnp.int32, (Q_BLK, 1), 0)
        cnt = jnp.minimum(w_lane, pos + 1).astype(F32)
        return (jnp.concatenate(es, axis=0), rinv, jnp.concatenate(ys, axis=0), _pool_diff(sums, cnt, cur))

    def tail(blk, p, rinv, y, d):
        r0 = blk * Q_BLK
        o = jnp.dot(p, vx_ref[r0:r0 + WINDOW + Q_BLK, :], preferred_element_type=F32)
        o_heads = [o[h * Q_BLK:(h + 1) * Q_BLK, :] * rinv[h] for h in range(N_HEADS)]
        mix_ref[r0:r0 + Q_BLK, 0:ATTN_DIM] = _unstack_heads(o_heads).astype(mix_ref.dtype)
        conv = jnp.dot(y, pw_ref[...], preferred_element_type=F32)
        mix_ref[r0:r0 + Q_BLK, CONV_OUT_OFF:CONV_OUT_OFF + CONV_DIM] = conv.astype(mix_ref.dtype)
        pool = jnp.dot(d, poolw_ref[...], preferred_element_type=F32) * pools_ref[...]
        mix_ref[r0:r0 + Q_BLK, POOL_OUT_OFF:POOL_OUT_OFF + POOL_DIM] = pool.astype(mix_ref.dtype)

    def finish():
        nk_ref[...] = z_ref[tm - WINDOW:, K_OFF:K_OFF + KV_DIM]
        nv_ref[...] = z_ref[tm - WINDOW:, V_OFF:V_OFF + KV_DIM]
        nc_ref[...] = gx_ref[CONV_HALO + tm - (CONV_K - 1):, :]
        np_ref[...] = px_ref[POOL_HALO + tm - (POOL_MAX - 1):, :]
        kx_ref[0:WINDOW, :] = kx_ref[tm:tm + WINDOW, :]
        vx_ref[0:WINDOW, :] = vx_ref[tm:tm + WINDOW, :]
        gx_ref[0:CONV_HALO, :] = gx_ref[tm:tm + CONV_HALO, :]
        px_ref[0:POOL_HALO, :] = px_ref[tm:tm + POOL_HALO, :]

    pending = {}

    def run_head(blk):
        pending[blk] = head(blk)

    def run_tail(blk):
        tail(blk, *pending.pop(blk))

    blocks = range(tm // Q_BLK)
    return ([functools.partial(run_head, blk) for blk in blocks],
            [functools.partial(run_tail, blk) for blk in blocks], finish)


def _mixer_weight_specs(l):
    return [pl.BlockSpec(memory_space=pltpu.SMEM),
            _layer_spec(l, (CONV_K, CONV_DIM)), _layer_spec(l, (1, CONV_DIM)),
            _layer_spec(l, (1, CONV_DIM)), _layer_spec(l, (1, CONV_DIM)),
            _layer_spec(l, (CONV_DIM, CONV_DIM)), _layer_spec(l, (POOL_DIM, POOL_DIM)),
            _layer_spec(l, (1, POOL_DIM))]


PROMPT_TILES = BATCH * SEQ // TM_MIX
TILES_PER_SEQ = SEQ // TM_MIX


def _prompt_mix_out_ffn_kernel(sinks_ref, z_ref, h_ref, dww_ref, dwb_ref, lng_ref, lnb_ref, pw_ref, poolw_ref,
                               pools_ref, ng_ref, wout_ref, wg_ref, wu_ref, wd_ref,
                               y_ref, nk_ref, nv_ref, nc_ref, np_ref,
                               kx_ref, vx_ref, gx_ref, px_ref, mix_ref, h2_ref, hn_ref, *, layer):
    s = pl.program_id(0)
    i = lax.rem(jnp.minimum(s, PROMPT_TILES - 1), TILES_PER_SEQ)

    @pl.when(s == 0)
    def _():
        h2_ref[...] = jnp.zeros(h2_ref.shape, h2_ref.dtype)
        hn_ref[...] = jnp.zeros(hn_ref.shape, hn_ref.dtype)

    @pl.when(i == 0)
    def _():
        kx_ref[0:WINDOW, :] = jnp.zeros((WINDOW, KV_DIM), BF16)
        vx_ref[0:WINDOW, :] = jnp.zeros((WINDOW, KV_DIM), BF16)
        gx_ref[0:CONV_HALO, :] = jnp.zeros((CONV_HALO, CONV_DIM), F32)
        px_ref[0:POOL_HALO, :] = jnp.zeros((POOL_HALO, POOL_DIM), F32)

    heads, tails, finish = _prompt_mixer_pieces(i, sinks_ref, z_ref, dww_ref, dwb_ref, lng_ref, lnb_ref, pw_ref,
                                                poolw_ref, pools_ref, mix_ref, nk_ref, nv_ref, nc_ref, np_ref,
                                                kx_ref, vx_ref, gx_ref, px_ref, layer)
    ffn = _swiglu(hn_ref[...], wg_ref, wu_ref, wd_ref, FF_SPLITS_FUSED, [None] + heads, [None] + tails)
    finish()
    y_ref[...] = h2_ref[...] + 0.5 * _rms(ffn, ng_ref[5:6, :])
    m = jnp.dot(mix_ref[...], wout_ref[...], preferred_element_type=F32)
    h2 = h_ref[...] + _rms(m, ng_ref[3:4, :])
    h2_ref[...] = h2
    hn_ref[...] = _rms(h2, ng_ref[4:5, :]).astype(BF16)


def _prompt_mix_out_ffn(l, z, h, sinks, dww, dwb, lng, lnb, pw, poolw, pools, ng, wout, wg, wu, wd):
    last = PROMPT_TILES - 1
    cur = lambda s: (jnp.minimum(s, last), 0)
    prev = lambda s: (jnp.maximum(s - 1, 0), 0)
    seq3 = lambda s: (jnp.minimum(s, last) // TILES_PER_SEQ, 0, 0)
    wspecs = _mixer_weight_specs(l)
    return pl.pallas_call(
        functools.partial(_prompt_mix_out_ffn_kernel, layer=l),
        grid=(PROMPT_TILES + 1,),
        in_specs=[wspecs[0], pl.BlockSpec((TM_MIX, IN_DIM), cur), pl.BlockSpec((TM_MIX, D_MODEL), cur)]
                 + wspecs[1:]
                 + [_layer_spec(l, (6, D_MODEL), True), _layer_spec(l, (MIX_DIM, D_MODEL), True),
                    _layer_spec(l, (D_MODEL, D_FF), True), _layer_spec(l, (D_MODEL, D_FF), True),
                    _layer_spec(l, (D_FF, D_MODEL), True)],
        out_specs=[pl.BlockSpec((TM_MIX, D_MODEL), prev),
                   pl.BlockSpec((None, WINDOW, KV_DIM), seq3), pl.BlockSpec((None, WINDOW, KV_DIM), seq3),
                   pl.BlockSpec((None, CONV_K - 1, CONV_DIM), seq3),
                   pl.BlockSpec((None, POOL_MAX - 1, POOL_DIM), seq3)],
        out_shape=[jax.ShapeDtypeStruct((BATCH * SEQ, D_MODEL), F32),
                   jax.ShapeDtypeStruct((BATCH, WINDOW, KV_DIM), F32),
                   jax.ShapeDtypeStruct((BATCH, WINDOW, KV_DIM), F32),
                   jax.ShapeDtypeStruct((BATCH, CONV_K - 1, CONV_DIM), F32),
                   jax.ShapeDtypeStruct((BATCH, POOL_MAX - 1, POOL_DIM), F32)],
        scratch_shapes=[pltpu.VMEM((WINDOW + TM_MIX, KV_DIM), BF16),
                        pltpu.VMEM((WINDOW + TM_MIX, KV_DIM), BF16),
                        pltpu.VMEM((CONV_HALO + TM_MIX, CONV_DIM), F32),
                        pltpu.VMEM((POOL_HALO + TM_MIX, POOL_DIM), F32),
                        pltpu.VMEM((TM_MIX, MIX_DIM), BF16),
                        pltpu.VMEM((TM_MIX, D_MODEL), F32),
                        pltpu.VMEM((TM_MIX, D_MODEL), BF16)],
        compiler_params=pltpu.CompilerParams(dimension_semantics=("arbitrary",),
                                             vmem_limit_bytes=VMEM_LIMIT_FUSED),
        name="prompt_mix_out_ffn",
    )(sinks, z, h, dww, dwb, lng, lnb, pw, poolw, pools, ng, wout, wg, wu, wd)


def _sample_mixer_kernel(sinks_ref, z_ref, sk_ref, sv_ref, sc_ref, sp_ref,
                         dww_ref, dwb_ref, lng_ref, lnb_ref, pw_ref, poolw_ref, pools_ref,
                         mix_ref, nk_ref, nv_ref, nc_ref, np_ref,
                         kx_ref, vx_ref, gx_ref, px_ref, *, layer):
    nb, t = SEQ_BLK, DEC_SEQ
    k_new = z_ref[:, :, K_OFF:K_OFF + KV_DIM]
    v_new = z_ref[:, :, V_OFF:V_OFF + KV_DIM]

    kx_ref[:, 0:WINDOW, :] = sk_ref[...].astype(BF16)
    vx_ref[:, 0:WINDOW, :] = sv_ref[...].astype(BF16)
    pad_rows = S_KEYS - WINDOW
    kx_ref[:, WINDOW:, :] = jnp.concatenate(
        [k_new, jnp.zeros((nb, pad_rows - t, KV_DIM), F32)], axis=1).astype(BF16)
    vx_ref[:, WINDOW:, :] = jnp.concatenate(
        [v_new, jnp.zeros((nb, pad_rows - t, KV_DIM), F32)], axis=1).astype(BF16)

    q_io = lax.broadcasted_iota(jnp.int32, (t, S_KEYS), 0)
    s_io = lax.broadcasted_iota(jnp.int32, (t, S_KEYS), 1)
    dist = q_io + WINDOW - s_io
    valid = (dist >= 0) & (dist < WINDOW) & (s_io + (PAST_LEN - WINDOW) >= 0)
    bias = _alibi_bias(dist.astype(F32))
    sinks = [sinks_ref[layer, h] * LOG2E for h in range(N_HEADS)]

    qs = _stack_q_heads(z_ref[:, :, Q_OFF:Q_OFF + ATTN_DIM])
    s = jnp.einsum('bqd,bkd->bqk', qs, kx_ref[...], preferred_element_type=F32)
    es, rinv = _softmax_heads(s, valid, bias, sinks, t)
    o = jnp.einsum('bqk,bkd->bqd', jnp.concatenate(es, axis=1), vx_ref[...], preferred_element_type=F32)
    o_heads = [o[:, h * t:(h + 1) * t, :] * rinv[h] for h in range(N_HEADS)]
    attn = _unstack_heads(o_heads)
    mix_ref[:, 0:ATTN_DIM] = attn.reshape(nb * t, ATTN_DIM)

    nk_ref[:, 0:WINDOW - t, :] = sk_ref[:, t:, :]
    nk_ref[:, WINDOW - t:, :] = k_new
    nv_ref[:, 0:WINDOW - t, :] = sv_ref[:, t:, :]
    nv_ref[:, WINDOW - t:, :] = v_new

    hist = CONV_K - 1
    gx_ref[:, 0:hist, :] = sc_ref[...]
    gx_ref[:, hist:hist + t, :] = (z_ref[:, :, GA_OFF:GA_OFF + CONV_DIM]
                                   * jax.nn.sigmoid(z_ref[:, :, GB_OFF:GB_OFF + CONV_DIM]))
    c = jnp.broadcast_to(dwb_ref[...], (nb, t, CONV_DIM))
    for j in range(CONV_K):
        c = c + dww_ref[j:j + 1, :] * gx_ref[:, j:j + t, :]
    conv = _conv_tail(c.reshape(nb * t, CONV_DIM), lng_ref, lnb_ref, pw_ref)
    mix_ref[:, CONV_OUT_OFF:CONV_OUT_OFF + CONV_DIM] = conv
    nc_ref[...] = gx_ref[:, t:t + hist, :]

    hist = POOL_MAX - 1
    px_ref[:, 0:hist, :] = sp_ref[...]
    px_ref[:, hist:hist + t, :] = z_ref[:, :, PU_OFF:PU_OFF + POOL_DIM]
    cur = px_ref[:, hist:hist + t, :]
    run = cur
    sums = []
    for back in range(1, POOL_MAX):
        run = run + px_ref[:, hist - back:hist - back + t, :]
        if back + 1 in POOL_WINDOWS:
            sums.append(run)
    pos = PAST_LEN + lax.broadcasted_iota(jnp.int32, (t, 1), 0)
    cnt = jnp.minimum(_pool_window_lane(), pos + 1).astype(F32)
    sums = [x.reshape(nb * t, POOL_DIM) for x in sums]
    cnt = jnp.broadcast_to(cnt, (nb, t, POOL_DIM)).reshape(nb * t, POOL_DIM)
    pool = _pool_tail(sums, cnt, cur.reshape(nb * t, POOL_DIM), poolw_ref, pools_ref)
    mix_ref[:, POOL_OUT_OFF:POOL_OUT_OFF + POOL_DIM] = pool
    np_ref[...] = px_ref[:, t:t + hist, :]


def _sample_mixer(l, z, sk, sv, sc, sp, sinks, dww, dwb, lng, lnb, pw, poolw, pools):
    nb = SEQ_BLK
    seq3 = lambda j: (j, 0, 0)
    lseq = lambda j: (l, j, 0, 0)
    wspecs = _mixer_weight_specs(l)
    return pl.pallas_call(
        functools.partial(_sample_mixer_kernel, layer=l),
        grid=(DEC_BATCH // nb,),
        in_specs=[wspecs[0],
                  pl.BlockSpec((nb, DEC_SEQ, IN_DIM), seq3),
                  pl.BlockSpec((None, nb, WINDOW, KV_DIM), lseq), pl.BlockSpec((None, nb, WINDOW, KV_DIM), lseq),
                  pl.BlockSpec((None, nb, CONV_K - 1, CONV_DIM), lseq),
                  pl.BlockSpec((None, nb, POOL_MAX - 1, POOL_DIM), lseq)] + wspecs[1:],
        out_specs=[pl.BlockSpec((nb * DEC_SEQ, MIX_DIM), lambda j: (j, 0)),
                   pl.BlockSpec((nb, WINDOW, KV_DIM), seq3), pl.BlockSpec((nb, WINDOW, KV_DIM), seq3),
                   pl.BlockSpec((nb, CONV_K - 1, CONV_DIM), seq3),
                   pl.BlockSpec((nb, POOL_MAX - 1, POOL_DIM), seq3)],
        out_shape=[jax.ShapeDtypeStruct((DEC_BATCH * DEC_SEQ, MIX_DIM), F32),
                   jax.ShapeDtypeStruct((DEC_BATCH, WINDOW, KV_DIM), F32),
                   jax.ShapeDtypeStruct((DEC_BATCH, WINDOW, KV_DIM), F32),
                   jax.ShapeDtypeStruct((DEC_BATCH, CONV_K - 1, CONV_DIM), F32),
                   jax.ShapeDtypeStruct((DEC_BATCH, POOL_MAX - 1, POOL_DIM), F32)],
        scratch_shapes=[pltpu.VMEM((nb, S_KEYS, KV_DIM), BF16),
                        pltpu.VMEM((nb, S_KEYS, KV_DIM), BF16),
                        pltpu.VMEM((nb, CONV_K - 1 + DEC_SEQ + 2, CONV_DIM), F32),
                        pltpu.VMEM((nb, POOL_MAX - 1 + DEC_SEQ + 1, POOL_DIM), F32)],
        compiler_params=pltpu.CompilerParams(dimension_semantics=("arbitrary",),
                                             vmem_limit_bytes=VMEM_LIMIT),
        name="sample_mixer",
    )(sinks, z, sk, sv, sc, sp, dww, dwb, lng, lnb, pw, poolw, pools)


def _block_diag(pool_w):
    eye = jnp.eye(N_POOL_GROUPS, dtype=pool_w.dtype)
    return jnp.einsum('lgcd,gh->lgchd', pool_w, eye).reshape(DEPTH, POOL_DIM, POOL_DIM)


def kernel(x_prompt, x_sample, state_attn_k, state_attn_v, state_conv, state_pool, norm_g, ffn1_wg, ffn1_wu, ffn1_wd, w_in, attn_sinks, conv_dw_w, conv_dw_b, conv_ln_g, conv_ln_b, conv_pw_w, pool_w, pool_scale, w_out, ffn2_wg, ffn2_wu, ffn2_wd):
    hp = x_prompt.reshape(BATCH * SEQ, D_MODEL)
    hs = x_sample.reshape(DEC_BATCH * DEC_SEQ, D_MODEL)
    sk_all = state_attn_k.reshape(DEPTH, DEC_BATCH, WINDOW, KV_DIM)
    sv_all = state_attn_v.reshape(DEPTH, DEC_BATCH, WINDOW, KV_DIM)
    f1 = (ffn1_wg.astype(BF16), ffn1_wu.astype(BF16), ffn1_wd.astype(BF16))
    f2 = (ffn2_wg.astype(BF16), ffn2_wu.astype(BF16), ffn2_wd.astype(BF16))
    win = w_in.astype(BF16)
    wout = w_out.astype(BF16)
    mixer_w = (attn_sinks, conv_dw_w, conv_dw_b.reshape(DEPTH, 1, CONV_DIM),
               conv_ln_g.reshape(DEPTH, 1, CONV_DIM), conv_ln_b.reshape(DEPTH, 1, CONV_DIM),
               conv_pw_w.astype(BF16), _block_diag(pool_w).astype(BF16),
               pool_scale.reshape(DEPTH, 1, POOL_DIM))
    outs = [[] for _ in range(8)]
    for l in range(DEPTH):
        h1p, zp = _ffn_in(l, hp, norm_g, *f1, win)
        h1s, zs = _ffn_in(l, hs, norm_g, *f1, win)
        hp, k1, v1, c1, p1 = _prompt_mix_out_ffn(l, zp, h1p, *mixer_w, norm_g, wout, *f2)
        mixs, k2, v2, c2, p2 = _sample_mixer(l, zs.reshape(DEC_BATCH, DEC_SEQ, IN_DIM), sk_all, sv_all,
                                             state_conv, state_pool, *mixer_w)
        hs = _out_ffn(l, h1s, mixs, norm_g, wout, *f2)
        for lst, val in zip(outs, (k1, v1, c1, p1, k2, v2, c2, p2)):
            lst.append(val)

    kv_p = (DEPTH, BATCH, WINDOW, N_KV_HEADS, HEAD_DIM)
    kv_s = (DEPTH, DEC_BATCH, WINDOW, N_KV_HEADS, HEAD_DIM)
    return (hp.reshape(BATCH, SEQ, D_MODEL), hs.reshape(DEC_BATCH, DEC_SEQ, D_MODEL),
            jnp.stack(outs[0]).reshape(kv_p), jnp.stack(outs[1]).reshape(kv_p),
            jnp.stack(outs[2]), jnp.stack(outs[3]),
            jnp.stack(outs[4]).reshape(kv_s), jnp.stack(outs[5]).reshape(kv_s),
            jnp.stack(outs[6]), jnp.stack(outs[7]))
```

```python
import functools

import numpy as np
import jax
import jax.numpy as jnp
from jax import lax
from jax.experimental import pallas as pl
from jax.experimental.pallas import tpu as pltpu

D_MODEL = 1024
BATCH = 8
SEQ = 2048
DEPTH = 4
DEC_BATCH = 128
DEC_SEQ = 8
PAST_LEN = 8192

N_HEADS = 8
HEAD_DIM = 64
N_KV_HEADS = 2
GROUP = N_HEADS // N_KV_HEADS
ATTN_DIM = N_HEADS * HEAD_DIM
KV_DIM = N_KV_HEADS * HEAD_DIM
WINDOW = 128
CONV_DIM = D_MODEL // 4
CONV_K = 31
POOL_DIM = D_MODEL // 4
POOL_WINDOWS = (2, 4, 8, 16)
N_POOL_GROUPS = 4
POOL_GROUP_DIM = POOL_DIM // N_POOL_GROUPS
POOL_MAX = 16
IN_DIM = ATTN_DIM + 2 * KV_DIM + 2 * CONV_DIM + POOL_DIM
MIX_DIM = ATTN_DIM + CONV_DIM + POOL_DIM
D_FF = ((8 * D_MODEL // 3 + 127) // 128) * 128
RMS_EPS = 1e-6
LN_EPS = 1e-5
NEG_INF = -1e30

Q_OFF = 0
K_OFF = ATTN_DIM
V_OFF = K_OFF + KV_DIM
GA_OFF = V_OFF + KV_DIM
GB_OFF = GA_OFF + CONV_DIM
PU_OFF = GB_OFF + CONV_DIM
CONV_OUT_OFF = ATTN_DIM
POOL_OUT_OFF = ATTN_DIM + CONV_DIM

LANES = 128
SUBLANES = 8
TM_FFN = 512
MXU_TILE = 256
FF_SPLITS = (4 * MXU_TILE, 4 * MXU_TILE, 3 * MXU_TILE)
FF_SPLITS_FUSED = (2 * MXU_TILE, 2 * MXU_TILE, 2 * MXU_TILE, 2 * MXU_TILE, 3 * MXU_TILE)
TM_MIX = 512
Q_BLK = 128
CONV_ROWS = 64
CONV_HALO = 32
POOL_HALO = 16
SEQ_BLK = 16
S_KEYS = 256
VMEM_LIMIT = 48 * 1024 * 1024
VMEM_LIMIT_FUSED = 54 * 1024 * 1024

LOG2E = float(np.log2(np.e))
ALIBI_SLOPES2 = [float(s) * LOG2E
                 for s in np.exp2(-8.0 * np.arange(1, N_HEADS + 1, dtype=np.float32) / N_HEADS)]
Q_SCALE2 = HEAD_DIM ** -0.5 * LOG2E

F32 = jnp.float32
BF16 = jnp.bfloat16


def _rms(x, g):
    return x * lax.rsqrt(jnp.mean(x * x, axis=-1, keepdims=True) + RMS_EPS) * g


def _swiglu(xn, wg_ref, wu_ref, wd_ref, splits=(D_FF,), before=(), after=()):
    bounds = np.cumsum((0,) + tuple(splits))
    acc = None
    gate_up = None
    for c in range(len(splits) + 1):
        if c < len(splits):
            if c < len(before) and before[c] is not None:
                before[c]()
            sl = slice(int(bounds[c]), int(bounds[c + 1]))
            nxt = (jnp.dot(xn, wg_ref[:, sl], preferred_element_type=F32),
                   jnp.dot(xn, wu_ref[:, sl], preferred_element_type=F32))
        if c > 0:
            g, u = gate_up
            sl = slice(int(bounds[c - 1]), int(bounds[c]))
            a = (g * jax.nn.sigmoid(g) * u).astype(BF16)
            part = jnp.dot(a, wd_ref[sl, :], preferred_element_type=F32)
            acc = part if acc is None else acc + part
            if c - 1 < len(after) and after[c - 1] is not None:
                after[c - 1]()
        gate_up = nxt
    return acc


def _ffn_in_kernel(x_ref, ng_ref, wg_ref, wu_ref, wd_ref, win_ref, h_ref, z_ref):
    x = x_ref[...]
    xn = _rms(x, ng_ref[0:1, :]).astype(BF16)
    h = x + 0.5 * _rms(_swiglu(xn, wg_ref, wu_ref, wd_ref, FF_SPLITS), ng_ref[1:2, :])
    h_ref[...] = h
    u = _rms(h, ng_ref[2:3, :]).astype(BF16)
    z_ref[...] = jnp.dot(u, win_ref[...], preferred_element_type=F32)


def _out_ffn_kernel(h_ref, mix_ref, ng_ref, wout_ref, wg_ref, wu_ref, wd_ref, y_ref,
                    splits=(D_FF,), before=(), after=()):
    m = jnp.dot(mix_ref[...].astype(BF16), wout_ref[...], preferred_element_type=F32)
    h = h_ref[...] + _rms(m, ng_ref[3:4, :])
    hn = _rms(h, ng_ref[4:5, :]).astype(BF16)
    y_ref[...] = h + 0.5 * _rms(_swiglu(hn, wg_ref, wu_ref, wd_ref, splits, before, after), ng_ref[5:6, :])


def _layer_spec(l, shape, single_buffer=False):
    index_map = lambda *_: (l,) + (0,) * len(shape)
    if single_buffer:
        return pl.BlockSpec((None,) + shape, index_map, pipeline_mode=pl.Buffered(1))
    return pl.BlockSpec((None,) + shape, index_map)


def _ffn_in(l, x, ng, wg, wu, wd, win):
    n = x.shape[0]
    row = lambda i: (i, 0)
    return pl.pallas_call(
        _ffn_in_kernel,
        grid=(n // TM_FFN,),
        in_specs=[pl.BlockSpec((TM_FFN, D_MODEL), row),
                  _layer_spec(l, (6, D_MODEL), True),
                  _layer_spec(l, (D_MODEL, D_FF), True), _layer_spec(l, (D_MODEL, D_FF), True),
                  _layer_spec(l, (D_FF, D_MODEL), True), _layer_spec(l, (D_MODEL, IN_DIM), True)],
        out_specs=[pl.BlockSpec((TM_FFN, D_MODEL), row), pl.BlockSpec((TM_FFN, IN_DIM), row)],
        out_shape=[jax.ShapeDtypeStruct((n, D_MODEL), F32), jax.ShapeDtypeStruct((n, IN_DIM), F32)],
        compiler_params=pltpu.CompilerParams(dimension_semantics=("arbitrary",),
                                             vmem_limit_bytes=VMEM_LIMIT),
        name="ffn_in",
    )(x, ng, wg, wu, wd, win)


def _out_ffn(l, h, mix, ng, wout, wg, wu, wd):
    n = h.shape[0]
    row = lambda i: (i, 0)
    return pl.pallas_call(
        _out_ffn_kernel,
        grid=(n // TM_FFN,),
        in_specs=[pl.BlockSpec((TM_FFN, D_MODEL), row), pl.BlockSpec((TM_FFN, MIX_DIM), row),
                  _layer_spec(l, (6, D_MODEL), True), _layer_spec(l, (MIX_DIM, D_MODEL), True),
                  _layer_spec(l, (D_MODEL, D_FF), True), _layer_spec(l, (D_MODEL, D_FF), True),
                  _layer_spec(l, (D_FF, D_MODEL), True)],
        out_specs=pl.BlockSpec((TM_FFN, D_MODEL), row),
        out_shape=jax.ShapeDtypeStruct((n, D_MODEL), F32),
        compiler_params=pltpu.CompilerParams(dimension_semantics=("arbitrary",),
                                             vmem_limit_bytes=VMEM_LIMIT),
        name="out_ffn",
    )(h, mix, ng, wout, wg, wu, wd)


def _stack_q_heads(q):
    lane = lax.broadcasted_iota(jnp.int32, q.shape[:-1] + (LANES,), q.ndim - 1)
    pieces = []
    for h in range(N_HEADS):
        kvh = h // GROUP
        slab = q[..., (h // 2) * LANES:(h // 2 + 1) * LANES]
        if h % 2 != kvh:
            slab = pltpu.roll(slab, HEAD_DIM, q.ndim - 1)
        keep = (lane >= kvh * HEAD_DIM) & (lane < (kvh + 1) * HEAD_DIM)
        pieces.append((jnp.where(keep, slab, 0.0) * Q_SCALE2).astype(BF16))
    return jnp.concatenate(pieces, axis=-2)


def _unstack_heads(o_heads):
    ndim = o_heads[0].ndim
    lane = lax.broadcasted_iota(jnp.int32, o_heads[0].shape, ndim - 1)
    slabs = []
    for j in range(N_HEADS // 2):
        a, b = o_heads[2 * j], o_heads[2 * j + 1]
        kvh = (2 * j) // GROUP
        if kvh == 0:
            b = pltpu.roll(b, HEAD_DIM, ndim - 1)
        else:
            a = pltpu.roll(a, HEAD_DIM, ndim - 1)
        slabs.append(jnp.where(lane < HEAD_DIM, a, b))
    return jnp.concatenate(slabs, axis=-1)


def _alibi_bias(dist_f):
    return [ALIBI_SLOPES2[h] * dist_f for h in range(N_HEADS)]


def _softmax_heads(s, valid, bias, sinks, rows):
    es, rinv = [], []
    for h in range(N_HEADS):
        sh = s[..., h * rows:(h + 1) * rows, :]
        l = jnp.where(valid, sh - bias[h], NEG_INF)
        m = jnp.maximum(jnp.max(l, axis=-1, keepdims=True), sinks[h])
        e = jnp.exp2(l - m)
        den = jnp.sum(e, axis=-1, keepdims=True) + jnp.exp2(sinks[h] - m)
        es.append(e.astype(BF16))
        rinv.append(1.0 / den)
    return es, rinv


def _conv_ln_act(c, lng_ref, lnb_ref):
    mu = jnp.mean(c, axis=-1, keepdims=True)
    xc = c - mu
    y = xc * lax.rsqrt(jnp.mean(xc * xc, axis=-1, keepdims=True) + LN_EPS) * lng_ref[...] + lnb_ref[...]
    return (y * jax.nn.sigmoid(y)).astype(BF16)


def _conv_tail(c, lng_ref, lnb_ref, pw_ref):
    return jnp.dot(_conv_ln_act(c, lng_ref, lnb_ref), pw_ref[...], preferred_element_type=F32)


def _pool_diff(sums, cnt, cur):
    lane = lax.broadcasted_iota(jnp.int32, cur.shape, cur.ndim - 1)
    wsum = sums[-1]
    for g in range(N_POOL_GROUPS - 2, -1, -1):
        wsum = jnp.where(lane < (g + 1) * POOL_GROUP_DIM, sums[g], wsum)
    return (wsum / cnt - cur).astype(BF16)


def _pool_tail(sums, cnt, cur, poolw_ref, pools_ref):
    return jnp.dot(_pool_diff(sums, cnt, cur), poolw_ref[...], preferred_element_type=F32) * pools_ref[...]


def _pool_window_lane():
    lane = lax.broadcasted_iota(jnp.int32, (1, POOL_DIM), 1)
    w = jnp.full((1, POOL_DIM), POOL_WINDOWS[-1], jnp.int32)
    for g in range(N_POOL_GROUPS - 2, -1, -1):
        w = jnp.where(lane < (g + 1) * POOL_GROUP_DIM, POOL_WINDOWS[g], w)
    return w


def _dw_conv_rows(gx_ref, dww_ref, r0, rows):
    first = CONV_HALO - (CONV_K - 1)
    out = None
    for r in range(SUBLANES):
        ext = rows if r == 0 else rows + SUBLANES
        acc = None
        for m in range((first + CONV_K - 1) // SUBLANES + 1):
            j = SUBLANES * m + r - first
            if j < 0 or j >= CONV_K:
                continue
            lo = r0 + SUBLANES * m
            term = dww_ref[j:j + 1, :] * gx_ref[lo:lo + ext, :]
            acc = term if acc is None else acc + term
        part = acc if r == 0 else acc[r:r + rows, :]
        out = part if out is None else out + part
    return out


def _prompt_mixer_pieces(i, sinks_ref, z_ref, dww_ref, dwb_ref, lng_ref, lnb_ref, pw_ref, poolw_ref, pools_ref,
                         mix_ref, nk_ref, nv_ref, nc_ref, np_ref,
                         kx_ref, vx_ref, gx_ref, px_ref, layer):
    tm = TM_MIX

    kx_ref[WINDOW:, :] = z_ref[:, K_OFF:K_OFF + KV_DIM].astype(BF16)
    vx_ref[WINDOW:, :] = z_ref[:, V_OFF:V_OFF + KV_DIM].astype(BF16)
    gx_ref[CONV_HALO:, :] = z_ref[:, GA_OFF:GA_OFF + CONV_DIM] * jax.nn.sigmoid(z_ref[:, GB_OFF:GB_OFF + CONV_DIM])
    px_ref[POOL_HALO:, :] = z_ref[:, PU_OFF:PU_OFF + POOL_DIM]

    q_io = lax.broadcasted_iota(jnp.int32, (Q_BLK, WINDOW + Q_BLK), 0)
    s_io = lax.broadcasted_iota(jnp.int32, (Q_BLK, WINDOW + Q_BLK), 1)
    dist = q_io + WINDOW - s_io
    band = (dist >= 0) & (dist < WINDOW)
    bias = _alibi_bias(dist.astype(F32))
    sinks = [sinks_ref[layer, h] * LOG2E for h in range(N_HEADS)]
    w_lane = _pool_window_lane()

    def head(blk):
        r0 = blk * Q_BLK
        first_key_pos = i * tm + r0 - WINDOW
        valid = band & (s_io + first_key_pos >= 0)
        qs = _stack_q_heads(z_ref[r0:r0 + Q_BLK, Q_OFF:Q_OFF + ATTN_DIM])
        kb = kx_ref[r0:r0 + WINDOW + Q_BLK, :]
        s = lax.dot_general(qs, kb, (((1,), (1,)), ((), ())), preferred_element_type=F32)
        es, rinv = _softmax_heads(s, valid, bias, sinks, Q_BLK)
        ys = []
        for rc in range(Q_BLK // CONV_ROWS):
            c = _dw_conv_rows(gx_ref, dww_ref, r0 + rc * CONV_ROWS, CONV_ROWS) + dwb_ref[...]
            ys.append(_conv_ln_act(c, lng_ref, lnb_ref))
        run = px_ref[r0:r0 + POOL_HALO + Q_BLK, :]
        cur = run[POOL_HALO:, :]
        sums = []
        span = 1
        while span < POOL_MAX:
            run = run + pltpu.roll(run, span, 0)
            span *= 2
            if span in POOL_WINDOWS:
                sums.append(run[POOL_HALO:, :])
        pos = i * tm + r0 + lax.broadcasted_iota(jnp.int32, (Q_BLK, 1), 0)
        cnt = jnp.minimum(w_lane, pos + 1).astype(F32)
        return (jnp.concatenate(es, axis=0), rinv, jnp.concatenate(ys, axis=0), _pool_diff(sums, cnt, cur))

    def tail(blk, p, rinv, y, d):
        r0 = blk * Q_BLK
        o = jnp.dot(p, vx_ref[r0:r0 + WINDOW + Q_BLK, :], preferred_element_type=F32)
        o_heads = [o[h * Q_BLK:(h + 1) * Q_BLK, :] * rinv[h] for h in range(N_HEADS)]
        mix_ref[r0:r0 + Q_BLK, 0:ATTN_DIM] = _unstack_heads(o_heads).astype(mix_ref.dtype)
        conv = jnp.dot(y, pw_ref[...], preferred_element_type=F32)
        mix_ref[r0:r0 + Q_BLK, CONV_OUT_OFF:CONV_OUT_OFF + CONV_DIM] = conv.astype(mix_ref.dtype)
        pool = jnp.dot(d, poolw_ref[...], preferred_element_type=F32) * pools_ref[...]
        mix_ref[r0:r0 + Q_BLK, POOL_OUT_OFF:POOL_OUT_OFF + POOL_DIM] = pool.astype(mix_ref.dtype)

    def finish():
        nk_ref[...] = z_ref[tm - WINDOW:, K_OFF:K_OFF + KV_DIM]
        nv_ref[...] = z_ref[tm - WINDOW:, V_OFF:V_OFF + KV_DIM]
        nc_ref[...] = gx_ref[CONV_HALO + tm - (CONV_K - 1):, :]
        np_ref[...] = px_ref[POOL_HALO + tm - (POOL_MAX - 1):, :]
        kx_ref[0:WINDOW, :] = kx_ref[tm:tm + WINDOW, :]
        vx_ref[0:WINDOW, :] = vx_ref[tm:tm + WINDOW, :]
        gx_ref[0:CONV_HALO, :] = gx_ref[tm:tm + CONV_HALO, :]
        px_ref[0:POOL_HALO, :] = px_ref[tm:tm + POOL_HALO, :]

    pending = {}

    def run_head(blk):
        pending[blk] = head(blk)

    def run_tail(blk):
        tail(blk, *pending.pop(blk))

    blocks = range(tm // Q_BLK)
    return ([functools.partial(run_head, blk) for blk in blocks],
            [functools.partial(run_tail, blk) for blk in blocks], finish)


def _mixer_weight_specs(l):
    return [pl.BlockSpec(memory_space=pltpu.SMEM),
            _layer_spec(l, (CONV_K, CONV_DIM)), _layer_spec(l, (1, CONV_DIM)),
            _layer_spec(l, (1, CONV_DIM)), _layer_spec(l, (1, CONV_DIM)),
            _layer_spec(l, (CONV_DIM, CONV_DIM)), _layer_spec(l, (POOL_DIM, POOL_DIM)),
            _layer_spec(l, (1, POOL_DIM))]


PROMPT_TILES = BATCH * SEQ // TM_MIX
TILES_PER_SEQ = SEQ // TM_MIX


def _prompt_mix_out_ffn_kernel(sinks_ref, z_ref, h_ref, dww_ref, dwb_ref, lng_ref, lnb_ref, pw_ref, poolw_ref,
                               pools_ref, ng_ref, wout_ref, wg_ref, wu_ref, wd_ref,
                               y_ref, nk_ref, nv_ref, nc_ref, np_ref,
                               kx_ref, vx_ref, gx_ref, px_ref, mix_ref, h2_ref, hn_ref, *, layer):
    s = pl.program_id(0)
    i = lax.rem(jnp.minimum(s, PROMPT_TILES - 1), TILES_PER_SEQ)

    @pl.when(s == 0)
    def _():
        h2_ref[...] = jnp.zeros(h2_ref.shape, h2_ref.dtype)
        hn_ref[...] = jnp.zeros(hn_ref.shape, hn_ref.dtype)

    @pl.when(i == 0)
    def _():
        kx_ref[0:WINDOW, :] = jnp.zeros((WINDOW, KV_DIM), BF16)
        vx_ref[0:WINDOW, :] = jnp.zeros((WINDOW, KV_DIM), BF16)
        gx_ref[0:CONV_HALO, :] = jnp.zeros((CONV_HALO, CONV_DIM), F32)
        px_ref[0:POOL_HALO, :] = jnp.zeros((POOL_HALO, POOL_DIM), F32)

    heads, tails, finish = _prompt_mixer_pieces(i, sinks_ref, z_ref, dww_ref, dwb_ref, lng_ref, lnb_ref, pw_ref,
                                                poolw_ref, pools_ref, mix_ref, nk_ref, nv_ref, nc_ref, np_ref,
                                                kx_ref, vx_ref, gx_ref, px_ref, layer)
    ffn = _swiglu(hn_ref[...], wg_ref, wu_ref, wd_ref, FF_SPLITS_FUSED, [None] + heads, tails)
    finish()
    y_ref[...] = h2_ref[...] + 0.5 * _rms(ffn, ng_ref[5:6, :])
    m = jnp.dot(mix_ref[...], wout_ref[...], preferred_element_type=F32)
    h2 = h_ref[...] + _rms(m, ng_ref[3:4, :])
    h2_ref[...] = h2
    hn_ref[...] = _rms(h2, ng_ref[4:5, :]).astype(BF16)


def _prompt_mix_out_ffn(l, z, h, sinks, dww, dwb, lng, lnb, pw, poolw, pools, ng, wout, wg, wu, wd):
    last = PROMPT_TILES - 1
    cur = lambda s: (jnp.minimum(s, last), 0)
    prev = lambda s: (jnp.maximum(s - 1, 0), 0)
    seq3 = lambda s: (jnp.minimum(s, last) // TILES_PER_SEQ, 0, 0)
    wspecs = _mixer_weight_specs(l)
    return pl.pallas_call(
        functools.partial(_prompt_mix_out_ffn_kernel, layer=l),
        grid=(PROMPT_TILES + 1,),
        in_specs=[wspecs[0], pl.BlockSpec((TM_MIX, IN_DIM), cur), pl.BlockSpec((TM_MIX, D_MODEL), cur)]
                 + wspecs[1:]
                 + [_layer_spec(l, (6, D_MODEL), True), _layer_spec(l, (MIX_DIM, D_MODEL), True),
                    _layer_spec(l, (D_MODEL, D_FF), True), _layer_spec(l, (D_MODEL, D_FF), True),
                    _layer_spec(l, (D_FF, D_MODEL), True)],
        out_specs=[pl.BlockSpec((TM_MIX, D_MODEL), prev),
                   pl.BlockSpec((None, WINDOW, KV_DIM), seq3), pl.BlockSpec((None, WINDOW, KV_DIM), seq3),
                   pl.BlockSpec((None, CONV_K - 1, CONV_DIM), seq3),
                   pl.BlockSpec((None, POOL_MAX - 1, POOL_DIM), seq3)],
        out_shape=[jax.ShapeDtypeStruct((BATCH * SEQ, D_MODEL), F32),
                   jax.ShapeDtypeStruct((BATCH, WINDOW, KV_DIM), F32),
                   jax.ShapeDtypeStruct((BATCH, WINDOW, KV_DIM), F32),
                   jax.ShapeDtypeStruct((BATCH, CONV_K - 1, CONV_DIM), F32),
                   jax.ShapeDtypeStruct((BATCH, POOL_MAX - 1, POOL_DIM), F32)],
        scratch_shapes=[pltpu.VMEM((WINDOW + TM_MIX, KV_DIM), BF16),
                        pltpu.VMEM((WINDOW + TM_MIX, KV_DIM), BF16),
                        pltpu.VMEM((CONV_HALO + TM_MIX, CONV_DIM), F32),
                        pltpu.VMEM((POOL_HALO + TM_MIX, POOL_DIM), F32),
                        pltpu.VMEM((TM_MIX, MIX_DIM), BF16),
                        pltpu.VMEM((TM_MIX, D_MODEL), F32),
                        pltpu.VMEM((TM_MIX, D_MODEL), BF16)],
        compiler_params=pltpu.CompilerParams(dimension_semantics=("arbitrary",),
                                             vmem_limit_bytes=VMEM_LIMIT_FUSED),
        name="prompt_mix_out_ffn",
    )(sinks, z, h, dww, dwb, lng, lnb, pw, poolw, pools, ng, wout, wg, wu, wd)


def _sample_mixer_kernel(sinks_ref, z3_ref, z2_ref, sk_ref, sv_ref, sc_ref, sp_ref,
                         dww_ref, dwb_ref, lng_ref, lnb_ref, pw_ref, poolw_ref, pools_ref,
                         ck_ref, cv_ref, cc_ref, cp_ref,
                         mix_ref, nk_ref, nv_ref, nc_ref, np_ref,
                         kx_ref, vx_ref, gx_ref, px_ref, re_ref, *, layer):
    del ck_ref, cv_ref, cc_ref, cp_ref
    nb, t = SEQ_BLK, DEC_SEQ
    lane = lax.broadcasted_iota(jnp.int32, (KV_DIM, LANES), 1)

    for x_ref, s_ref, n_ref, off in ((kx_ref, sk_ref, nk_ref, K_OFF), (vx_ref, sv_ref, nv_ref, V_OFF)):
        new_t = z2_ref[:, off:off + KV_DIM].T
        for b in range(nb):
            old = jnp.concatenate([s_ref[b, kv] for kv in range(N_KV_HEADS)], axis=0)
            x_ref[b, :, 0:WINDOW] = old.astype(BF16)
            fresh = pltpu.roll(new_t, (LANES - b * t) % LANES, 1)
            x_ref[b, :, WINDOW:] = jnp.where(lane < t, fresh, 0.0).astype(BF16)
            shifted = jnp.where(lane < WINDOW - t, pltpu.roll(old, WINDOW - t, 1),
                                pltpu.roll(new_t, (WINDOW - t - b * t) % LANES, 1))
            for kv in range(N_KV_HEADS):
                n_ref[b, kv] = shifted[kv * HEAD_DIM:(kv + 1) * HEAD_DIM, :]

    q_io = lax.broadcasted_iota(jnp.int32, (t, S_KEYS), 0)
    s_io = lax.broadcasted_iota(jnp.int32, (t, S_KEYS), 1)
    dist = q_io + WINDOW - s_io
    valid = (dist >= 0) & (dist < WINDOW) & (s_io + (PAST_LEN - WINDOW) >= 0)
    bias = _alibi_bias(dist.astype(F32))
    sinks = [sinks_ref[layer, h] * LOG2E for h in range(N_HEADS)]

    qs = _stack_q_heads(z3_ref[:, :, Q_OFF:Q_OFF + ATTN_DIM])
    s = jnp.einsum('bqd,bdk->bqk', qs, kx_ref[...], preferred_element_type=F32)
    es, rinv = _softmax_heads(s, valid, bias, sinks, t)
    o = jnp.einsum('bqk,bdk->bqd', jnp.concatenate(es, axis=1), vx_ref[...], preferred_element_type=F32)
    o_heads = [o[:, h * t:(h + 1) * t, :] * rinv[h] for h in range(N_HEADS)]
    attn = _unstack_heads(o_heads)
    mix_ref[:, 0:ATTN_DIM] = attn.reshape(nb * t, ATTN_DIM)

    def to_token_major(x, dst_ref, first):
        for cb in range(x.shape[1] // LANES):
            re_ref[cb] = x[:, cb * LANES:(cb + 1) * LANES]
        for tok in range(t):
            for cb in range(x.shape[1] // LANES):
                dst_ref[first + tok, :, cb * LANES:(cb + 1) * LANES] = re_ref[cb, pl.ds(tok, nb, stride=t), :]

    def to_seq_major(x):
        for tok in range(t):
            for cb in range(x.shape[1] // LANES):
                re_ref[cb, pl.ds(tok, nb, stride=t), :] = x[tok * nb:(tok + 1) * nb, cb * LANES:(cb + 1) * LANES]
        return jnp.concatenate([re_ref[cb] for cb in range(x.shape[1] // LANES)], axis=1)

    hist = CONV_K - 1
    gx_ref[0:hist] = sc_ref[...]
    to_token_major(z2_ref[:, GA_OFF:GA_OFF + CONV_DIM] * jax.nn.sigmoid(z2_ref[:, GB_OFF:GB_OFF + CONV_DIM]),
                   gx_ref, hist)
    nc_ref[...] = gx_ref[t:t + hist]
    ys = []
    for tok in range(t):
        c = jnp.broadcast_to(dwb_ref[...], (nb, CONV_DIM))
        for j in range(CONV_K):
            c = c + dww_ref[j:j + 1, :] * gx_ref[tok + j]
        ys.append(_conv_ln_act(c, lng_ref, lnb_ref))
    conv = jnp.dot(jnp.concatenate(ys, axis=0), pw_ref[...], preferred_element_type=F32)
    mix_ref[:, CONV_OUT_OFF:CONV_OUT_OFF + CONV_DIM] = to_seq_major(conv)

    hist = POOL_MAX - 1
    px_ref[0:hist] = sp_ref[...]
    to_token_major(z2_ref[:, PU_OFF:PU_OFF + POOL_DIM], px_ref, hist)
    np_ref[...] = px_ref[t:t + hist]
    w_lane = _pool_window_lane()
    ds = []
    for tok in range(t):
        cur = px_ref[hist + tok]
        run = cur
        sums = []
        for back in range(1, POOL_MAX):
            run = run + px_ref[hist + tok - back]
            if back + 1 in POOL_WINDOWS:
                sums.append(run)
        cnt = jnp.minimum(w_lane, PAST_LEN + tok + 1).astype(F32)
        ds.append(_pool_diff(sums, cnt, cur))
    pool = jnp.dot(jnp.concatenate(ds, axis=0), poolw_ref[...], preferred_element_type=F32) * pools_ref[...]
    mix_ref[:, POOL_OUT_OFF:POOL_OUT_OFF + POOL_DIM] = to_seq_major(pool)


def _sample_mixer(l, z, sk, sv, sc, sp, new_states, sinks, dww, dwb, lng, lnb, pw, poolw, pools):
    nb = SEQ_BLK
    assert nb * DEC_SEQ == LANES and S_KEYS == 2 * LANES and WINDOW == LANES
    kv_spec = pl.BlockSpec((None, nb, N_KV_HEADS, HEAD_DIM, WINDOW), lambda j: (l, j, 0, 0, 0))
    rows_spec = lambda rows, ch: pl.BlockSpec((None, rows, nb, ch), lambda j: (l, 0, j, 0))
    wspecs = _mixer_weight_specs(l)
    operands = (sinks, z.reshape(DEC_BATCH, DEC_SEQ, IN_DIM), z, sk, sv, sc, sp, dww, dwb, lng, lnb, pw, poolw, pools)
    carried = tuple(new_states)
    return pl.pallas_call(
        functools.partial(_sample_mixer_kernel, layer=l),
        grid=(DEC_BATCH // nb,),
        in_specs=[wspecs[0],
                  pl.BlockSpec((nb, DEC_SEQ, IN_DIM), lambda j: (j, 0, 0)),
                  pl.BlockSpec((nb * DEC_SEQ, IN_DIM), lambda j: (j, 0)),
                  kv_spec, kv_spec, rows_spec(CONV_K - 1, CONV_DIM), rows_spec(POOL_MAX - 1, POOL_DIM)]
                 + wspecs[1:] + [pl.BlockSpec(memory_space=pl.ANY)] * len(carried),
        out_specs=[pl.BlockSpec((nb * DEC_SEQ, MIX_DIM), lambda j: (j, 0)),
                   kv_spec, kv_spec, rows_spec(CONV_K - 1, CONV_DIM), rows_spec(POOL_MAX - 1, POOL_DIM)],
        out_shape=[jax.ShapeDtypeStruct((DEC_BATCH * DEC_SEQ, MIX_DIM), F32),
                   jax.ShapeDtypeStruct(sk.shape, F32), jax.ShapeDtypeStruct(sv.shape, F32),
                   jax.ShapeDtypeStruct(sc.shape, F32), jax.ShapeDtypeStruct(sp.shape, F32)],
        input_output_aliases={len(operands) + n: 1 + n for n in range(len(carried))},
        scratch_shapes=[pltpu.VMEM((nb, KV_DIM, S_KEYS), BF16),
                        pltpu.VMEM((nb, KV_DIM, S_KEYS), BF16),
                        pltpu.VMEM((CONV_K - 1 + DEC_SEQ, nb, CONV_DIM), F32),
                        pltpu.VMEM((POOL_MAX - 1 + DEC_SEQ, nb, POOL_DIM), F32),
                        pltpu.VMEM((max(CONV_DIM, POOL_DIM) // LANES, nb * DEC_SEQ, LANES), F32)],
        compiler_params=pltpu.CompilerParams(dimension_semantics=("arbitrary",),
                                             vmem_limit_bytes=VMEM_LIMIT),
        name="sample_mixer",
    )(*operands, *carried)


def _block_diag(pool_w):
    eye = jnp.eye(N_POOL_GROUPS, dtype=pool_w.dtype)
    return jnp.einsum('lgcd,gh->lgchd', pool_w, eye).reshape(DEPTH, POOL_DIM, POOL_DIM)


def kernel(x_prompt, x_sample, state_attn_k, state_attn_v, state_conv, state_pool, norm_g, ffn1_wg, ffn1_wu, ffn1_wd, w_in, attn_sinks, conv_dw_w, conv_dw_b, conv_ln_g, conv_ln_b, conv_pw_w, pool_w, pool_scale, w_out, ffn2_wg, ffn2_wu, ffn2_wd):
    hp = x_prompt.reshape(BATCH * SEQ, D_MODEL)
    hs = x_sample.reshape(DEC_BATCH * DEC_SEQ, D_MODEL)
    sk_all = state_attn_k.transpose(0, 1, 3, 4, 2)
    sv_all = state_attn_v.transpose(0, 1, 3, 4, 2)
    sc_all = state_conv.transpose(0, 2, 1, 3)
    sp_all = state_pool.transpose(0, 2, 1, 3)
    f1 = (ffn1_wg.astype(BF16), ffn1_wu.astype(BF16), ffn1_wd.astype(BF16))
    f2 = (ffn2_wg.astype(BF16), ffn2_wu.astype(BF16), ffn2_wd.astype(BF16))
    win = w_in.astype(BF16)
    wout = w_out.astype(BF16)
    mixer_w = (attn_sinks, conv_dw_w, conv_dw_b.reshape(DEPTH, 1, CONV_DIM),
               conv_ln_g.reshape(DEPTH, 1, CONV_DIM), conv_ln_b.reshape(DEPTH, 1, CONV_DIM),
               conv_pw_w.astype(BF16), _block_diag(pool_w).astype(BF16),
               pool_scale.reshape(DEPTH, 1, POOL_DIM))
    outs = [[] for _ in range(4)]
    sample_states = [jnp.zeros_like(a) for a in (sk_all, sv_all, sc_all, sp_all)]
    for l in range(DEPTH):
        h1p, zp = _ffn_in(l, hp, norm_g, *f1, win)
        h1s, zs = _ffn_in(l, hs, norm_g, *f1, win)
        hp, k1, v1, c1, p1 = _prompt_mix_out_ffn(l, zp, h1p, *mixer_w, norm_g, wout, *f2)
        mixs, *sample_states = _sample_mixer(l, zs, sk_all, sv_all, sc_all, sp_all, sample_states, *mixer_w)
        hs = _out_ffn(l, h1s, mixs, norm_g, wout, *f2)
        for lst, val in zip(outs, (k1, v1, c1, p1)):
            lst.append(val)

    kv_p = (DEPTH, BATCH, WINDOW, N_KV_HEADS, HEAD_DIM)
    k2, v2, c2, p2 = sample_states
    return (hp.reshape(BATCH, SEQ, D_MODEL), hs.reshape(DEC_BATCH, DEC_SEQ, D_MODEL),
            jnp.stack(outs[0]).reshape(kv_p), jnp.stack(outs[1]).reshape(kv_p),
            jnp.stack(outs[2]), jnp.stack(outs[3]),
            k2.transpose(0, 1, 4, 2, 3), v2.transpose(0, 1, 4, 2, 3),
            c2.transpose(0, 2, 1, 3), p2.transpose(0, 2, 1, 3))
```

```python
import functools

import numpy as np
import jax
import jax.numpy as jnp
from jax import lax
from jax.experimental import pallas as pl
from jax.experimental.pallas import tpu as pltpu

D_MODEL = 1024
BATCH = 8
SEQ = 2048
DEPTH = 4
DEC_BATCH = 128
DEC_SEQ = 8
PAST_LEN = 8192

N_HEADS = 8
HEAD_DIM = 64
N_KV_HEADS = 2
GROUP = N_HEADS // N_KV_HEADS
ATTN_DIM = N_HEADS * HEAD_DIM
KV_DIM = N_KV_HEADS * HEAD_DIM
WINDOW = 128
CONV_DIM = D_MODEL // 4
CONV_K = 31
POOL_DIM = D_MODEL // 4
POOL_WINDOWS = (2, 4, 8, 16)
N_POOL_GROUPS = 4
POOL_GROUP_DIM = POOL_DIM // N_POOL_GROUPS
POOL_MAX = 16
IN_DIM = ATTN_DIM + 2 * KV_DIM + 2 * CONV_DIM + POOL_DIM
MIX_DIM = ATTN_DIM + CONV_DIM + POOL_DIM
D_FF = ((8 * D_MODEL // 3 + 127) // 128) * 128
RMS_EPS = 1e-6
LN_EPS = 1e-5
NEG_INF = -1e30

Q_OFF = 0
K_OFF = ATTN_DIM
V_OFF = K_OFF + KV_DIM
GA_OFF = V_OFF + KV_DIM
GB_OFF = GA_OFF + CONV_DIM
PU_OFF = GB_OFF + CONV_DIM
CONV_OUT_OFF = ATTN_DIM
POOL_OUT_OFF = ATTN_DIM + CONV_DIM

LANES = 128
SUBLANES = 8
TM_FFN = 512
FFN_SUBTILES = 2
MXU_TILE = 256
FF_SPLITS = (4 * MXU_TILE, 4 * MXU_TILE, 3 * MXU_TILE)
FF_SPLITS_FUSED = (2 * MXU_TILE, 2 * MXU_TILE, 2 * MXU_TILE, 2 * MXU_TILE, 3 * MXU_TILE)
TM_MIX = 512
Q_BLK = 128
CONV_ROWS = 64
CONV_HALO = 32
POOL_HALO = 16
SEQ_BLK = 16
S_KEYS = 256
VMEM_LIMIT = 48 * 1024 * 1024
VMEM_LIMIT_FUSED = 54 * 1024 * 1024

LOG2E = float(np.log2(np.e))
ALIBI_SLOPES2 = [float(s) * LOG2E
                 for s in np.exp2(-8.0 * np.arange(1, N_HEADS + 1, dtype=np.float32) / N_HEADS)]
Q_SCALE2 = HEAD_DIM ** -0.5 * LOG2E

F32 = jnp.float32
BF16 = jnp.bfloat16


def _rms(x, g):
    return x * lax.rsqrt(jnp.mean(x * x, axis=-1, keepdims=True) + RMS_EPS) * g


def _swiglu(xn, wg_ref, wu_ref, wd_ref, splits=(D_FF,), before=(), after=()):
    bounds = np.cumsum((0,) + tuple(splits))
    acc = None
    gate_up = None
    for c in range(len(splits) + 1):
        if c < len(splits):
            if c < len(before) and before[c] is not None:
                before[c]()
            sl = slice(int(bounds[c]), int(bounds[c + 1]))
            nxt = (jnp.dot(xn, wg_ref[:, sl], preferred_element_type=F32),
                   jnp.dot(xn, wu_ref[:, sl], preferred_element_type=F32))
        if c > 0:
            g, u = gate_up
            sl = slice(int(bounds[c - 1]), int(bounds[c]))
            a = (g * jax.nn.sigmoid(g) * u).astype(BF16)
            part = jnp.dot(a, wd_ref[sl, :], preferred_element_type=F32)
            acc = part if acc is None else acc + part
            if c - 1 < len(after) and after[c - 1] is not None:
                after[c - 1]()
        gate_up = nxt
    return acc


def _ffn_in_kernel(xp_ref, xs_ref, ng_ref, wg_ref, wu_ref, wd_ref, win_ref, h_ref, z_ref):
    from_prompt = pl.program_id(0) < PROMPT_FFN_TILES
    sub = xp_ref.shape[0] // FFN_SUBTILES
    rows = [slice(k * sub, (k + 1) * sub) for k in range(FFN_SUBTILES)]
    xs = [jnp.where(from_prompt, xp_ref[r, :], xs_ref[r, :]) for r in rows]
    xns = [_rms(x, ng_ref[0:1, :]).astype(BF16) for x in xs]
    us = []
    for k in range(FFN_SUBTILES):
        ffn = _swiglu(xns[k], wg_ref, wu_ref, wd_ref, FF_SPLITS)
        h = xs[k] + 0.5 * _rms(ffn, ng_ref[1:2, :])
        h_ref[rows[k], :] = h
        us.append(_rms(h, ng_ref[2:3, :]).astype(BF16))
        if k > 0:
            z_ref[rows[k - 1], :] = jnp.dot(us[k - 1], win_ref[...], preferred_element_type=F32)
    z_ref[rows[-1], :] = jnp.dot(us[-1], win_ref[...], preferred_element_type=F32)


def _out_ffn_kernel(h_ref, mix_ref, ng_ref, wout_ref, wg_ref, wu_ref, wd_ref, y_ref,
                    splits=(D_FF,), before=(), after=()):
    m = jnp.dot(mix_ref[...].astype(BF16), wout_ref[...], preferred_element_type=F32)
    h = h_ref[...] + _rms(m, ng_ref[3:4, :])
    hn = _rms(h, ng_ref[4:5, :]).astype(BF16)
    y_ref[...] = h + 0.5 * _rms(_swiglu(hn, wg_ref, wu_ref, wd_ref, splits, before, after), ng_ref[5:6, :])


def _layer_spec(l, shape, single_buffer=False):
    index_map = lambda *_: (l,) + (0,) * len(shape)
    if single_buffer:
        return pl.BlockSpec((None,) + shape, index_map, pipeline_mode=pl.Buffered(1))
    return pl.BlockSpec((None,) + shape, index_map)


PROMPT_FFN_TILES = BATCH * SEQ // TM_FFN
SAMPLE_FFN_TILES = DEC_BATCH * DEC_SEQ // TM_FFN


def _ffn_in(l, xp, xs, ng, wg, wu, wd, win):
    n = xp.shape[0] + xs.shape[0]
    row = lambda i: (i, 0)
    return pl.pallas_call(
        _ffn_in_kernel,
        grid=(PROMPT_FFN_TILES + SAMPLE_FFN_TILES,),
        in_specs=[pl.BlockSpec((TM_FFN, D_MODEL), lambda i: (jnp.minimum(i, PROMPT_FFN_TILES - 1), 0)),
                  pl.BlockSpec((TM_FFN, D_MODEL), lambda i: (jnp.maximum(i - PROMPT_FFN_TILES, 0), 0)),
                  _layer_spec(l, (6, D_MODEL), True),
                  _layer_spec(l, (D_MODEL, D_FF), True), _layer_spec(l, (D_MODEL, D_FF), True),
                  _layer_spec(l, (D_FF, D_MODEL), True), _layer_spec(l, (D_MODEL, IN_DIM), True)],
        out_specs=[pl.BlockSpec((TM_FFN, D_MODEL), row), pl.BlockSpec((TM_FFN, IN_DIM), row)],
        out_shape=[jax.ShapeDtypeStruct((n, D_MODEL), F32), jax.ShapeDtypeStruct((n, IN_DIM), F32)],
        compiler_params=pltpu.CompilerParams(dimension_semantics=("arbitrary",),
                                             vmem_limit_bytes=VMEM_LIMIT),
        name="ffn_in",
    )(xp, xs, ng, wg, wu, wd, win)


def _sample_out_ffn(l, h, mix, ng, wout, wg, wu, wd):
    n = mix.shape[0]
    row = lambda i: (i, 0)
    return pl.pallas_call(
        _out_ffn_kernel,
        grid=(n // TM_FFN,),
        in_specs=[pl.BlockSpec((TM_FFN, D_MODEL), lambda i: (PROMPT_FFN_TILES + i, 0)),
                  pl.BlockSpec((TM_FFN, MIX_DIM), row),
                  _layer_spec(l, (6, D_MODEL), True), _layer_spec(l, (MIX_DIM, D_MODEL), True),
                  _layer_spec(l, (D_MODEL, D_FF), True), _layer_spec(l, (D_MODEL, D_FF), True),
                  _layer_spec(l, (D_FF, D_MODEL), True)],
        out_specs=pl.BlockSpec((TM_FFN, D_MODEL), row),
        out_shape=jax.ShapeDtypeStruct((n, D_MODEL), F32),
        compiler_params=pltpu.CompilerParams(dimension_semantics=("arbitrary",),
                                             vmem_limit_bytes=VMEM_LIMIT),
        name="out_ffn",
    )(h, mix, ng, wout, wg, wu, wd)


def _stack_q_heads(q):
    lane = lax.broadcasted_iota(jnp.int32, q.shape[:-1] + (LANES,), q.ndim - 1)
    pieces = []
    for h in range(N_HEADS):
        kvh = h // GROUP
        slab = q[..., (h // 2) * LANES:(h // 2 + 1) * LANES]
        if h % 2 != kvh:
            slab = pltpu.roll(slab, HEAD_DIM, q.ndim - 1)
        keep = (lane >= kvh * HEAD_DIM) & (lane < (kvh + 1) * HEAD_DIM)
        pieces.append((jnp.where(keep, slab, 0.0) * Q_SCALE2).astype(BF16))
    return jnp.concatenate(pieces, axis=-2)


def _unstack_heads(o_heads):
    ndim = o_heads[0].ndim
    lane = lax.broadcasted_iota(jnp.int32, o_heads[0].shape, ndim - 1)
    slabs = []
    for j in range(N_HEADS // 2):
        a, b = o_heads[2 * j], o_heads[2 * j + 1]
        kvh = (2 * j) // GROUP
        if kvh == 0:
            b = pltpu.roll(b, HEAD_DIM, ndim - 1)
        else:
            a = pltpu.roll(a, HEAD_DIM, ndim - 1)
        slabs.append(jnp.where(lane < HEAD_DIM, a, b))
    return jnp.concatenate(slabs, axis=-1)


def _alibi_bias(dist_f):
    return [ALIBI_SLOPES2[h] * dist_f for h in range(N_HEADS)]


def _softmax_heads(s, valid, bias, sinks, rows):
    es, rinv = [], []
    for h in range(N_HEADS):
        sh = s[..., h * rows:(h + 1) * rows, :]
        l = jnp.where(valid, sh - bias[h], NEG_INF)
        m = jnp.maximum(jnp.max(l, axis=-1, keepdims=True), sinks[h])
        e = jnp.exp2(l - m)
        den = jnp.sum(e, axis=-1, keepdims=True) + jnp.exp2(sinks[h] - m)
        es.append(e.astype(BF16))
        rinv.append(1.0 / den)
    return es, rinv


def _conv_ln_act(c, lng_ref, lnb_ref):
    mu = jnp.mean(c, axis=-1, keepdims=True)
    xc = c - mu
    y = xc * lax.rsqrt(jnp.mean(xc * xc, axis=-1, keepdims=True) + LN_EPS) * lng_ref[...] + lnb_ref[...]
    return (y * jax.nn.sigmoid(y)).astype(BF16)


def _conv_tail(c, lng_ref, lnb_ref, pw_ref):
    return jnp.dot(_conv_ln_act(c, lng_ref, lnb_ref), pw_ref[...], preferred_element_type=F32)


def _pool_diff(sums, cnt, cur):
    lane = lax.broadcasted_iota(jnp.int32, cur.shape, cur.ndim - 1)
    wsum = sums[-1]
    for g in range(N_POOL_GROUPS - 2, -1, -1):
        wsum = jnp.where(lane < (g + 1) * POOL_GROUP_DIM, sums[g], wsum)
    return (wsum / cnt - cur).astype(BF16)


def _pool_tail(sums, cnt, cur, poolw_ref, pools_ref):
    return jnp.dot(_pool_diff(sums, cnt, cur), poolw_ref[...], preferred_element_type=F32) * pools_ref[...]


def _pool_window_lane():
    lane = lax.broadcasted_iota(jnp.int32, (1, POOL_DIM), 1)
    w = jnp.full((1, POOL_DIM), POOL_WINDOWS[-1], jnp.int32)
    for g in range(N_POOL_GROUPS - 2, -1, -1):
        w = jnp.where(lane < (g + 1) * POOL_GROUP_DIM, POOL_WINDOWS[g], w)
    return w


def _dw_conv_rows(gx_ref, dww_ref, r0, rows):
    first = CONV_HALO - (CONV_K - 1)
    out = None
    for r in range(SUBLANES):
        ext = rows if r == 0 else rows + SUBLANES
        acc = None
        for m in range((first + CONV_K - 1) // SUBLANES + 1):
            j = SUBLANES * m + r - first
            if j < 0 or j >= CONV_K:
                continue
            lo = r0 + SUBLANES * m
            term = dww_ref[j:j + 1, :] * gx_ref[lo:lo + ext, :]
            acc = term if acc is None else acc + term
        part = acc if r == 0 else acc[r:r + rows, :]
        out = part if out is None else out + part
    return out


def _prompt_mixer_pieces(i, sinks_ref, z_ref, dww_ref, dwb_ref, lng_ref, lnb_ref, pw_ref, poolw_ref, pools_ref,
                         mix_ref, nk_ref, nv_ref, nc_ref, np_ref,
                         kx_ref, vx_ref, gx_ref, px_ref, layer):
    tm = TM_MIX

    kx_ref[WINDOW:, :] = z_ref[:, K_OFF:K_OFF + KV_DIM].astype(BF16)
    vx_ref[WINDOW:, :] = z_ref[:, V_OFF:V_OFF + KV_DIM].astype(BF16)
    gx_ref[CONV_HALO:, :] = z_ref[:, GA_OFF:GA_OFF + CONV_DIM] * jax.nn.sigmoid(z_ref[:, GB_OFF:GB_OFF + CONV_DIM])
    px_ref[POOL_HALO:, :] = z_ref[:, PU_OFF:PU_OFF + POOL_DIM]

    q_io = lax.broadcasted_iota(jnp.int32, (Q_BLK, WINDOW + Q_BLK), 0)
    s_io = lax.broadcasted_iota(jnp.int32, (Q_BLK, WINDOW + Q_BLK), 1)
    dist = q_io + WINDOW - s_io
    band = (dist >= 0) & (dist < WINDOW)
    bias = _alibi_bias(dist.astype(F32))
    sinks = [sinks_ref[layer, h] * LOG2E for h in range(N_HEADS)]
    w_lane = _pool_window_lane()

    def head(blk):
        r0 = blk * Q_BLK
        first_key_pos = i * tm + r0 - WINDOW
        valid = band & (s_io + first_key_pos >= 0)
        qs = _stack_q_heads(z_ref[r0:r0 + Q_BLK, Q_OFF:Q_OFF + ATTN_DIM])
        kb = kx_ref[r0:r0 + WINDOW + Q_BLK, :]
        s = lax.dot_general(qs, kb, (((1,), (1,)), ((), ())), preferred_element_type=F32)
        es, rinv = _softmax_heads(s, valid, bias, sinks, Q_BLK)
        ys = []
        for rc in range(Q_BLK // CONV_ROWS):
            c = _dw_conv_rows(gx_ref, dww_ref, r0 + rc * CONV_ROWS, CONV_ROWS) + dwb_ref[...]
            ys.append(_conv_ln_act(c, lng_ref, lnb_ref))
        run = px_ref[r0:r0 + POOL_HALO + Q_BLK, :]
        cur = run[POOL_HALO:, :]
        sums = []
        span = 1
        while span < POOL_MAX:
            run = run + pltpu.roll(run, span, 0)
            span *= 2
            if span in POOL_WINDOWS:
                sums.append(run[POOL_HALO:, :])
        pos = i * tm + r0 + lax.broadcasted_iota(jnp.int32, (Q_BLK, 1), 0)
        cnt = jnp.minimum(w_lane, pos + 1).astype(F32)
        return (jnp.concatenate(es, axis=0), rinv, jnp.concatenate(ys, axis=0), _pool_diff(sums, cnt, cur))

    def tail(blk, p, rinv, y, d):
        r0 = blk * Q_BLK
        o = jnp.dot(p, vx_ref[r0:r0 + WINDOW + Q_BLK, :], preferred_element_type=F32)
        o_heads = [o[h * Q_BLK:(h + 1) * Q_BLK, :] * rinv[h] for h in range(N_HEADS)]
        mix_ref[r0:r0 + Q_BLK, 0:ATTN_DIM] = _unstack_heads(o_heads).astype(mix_ref.dtype)
        conv = jnp.dot(y, pw_ref[...], preferred_element_type=F32)
        mix_ref[r0:r0 + Q_BLK, CONV_OUT_OFF:CONV_OUT_OFF + CONV_DIM] = conv.astype(mix_ref.dtype)
        pool = jnp.dot(d, poolw_ref[...], preferred_element_type=F32) * pools_ref[...]
        mix_ref[r0:r0 + Q_BLK, POOL_OUT_OFF:POOL_OUT_OFF + POOL_DIM] = pool.astype(mix_ref.dtype)

    def finish():
        nk_ref[...] = z_ref[tm - WINDOW:, K_OFF:K_OFF + KV_DIM]
        nv_ref[...] = z_ref[tm - WINDOW:, V_OFF:V_OFF + KV_DIM]
        nc_ref[...] = gx_ref[CONV_HALO + tm - (CONV_K - 1):, :]
        np_ref[...] = px_ref[POOL_HALO + tm - (POOL_MAX - 1):, :]
        kx_ref[0:WINDOW, :] = kx_ref[tm:tm + WINDOW, :]
        vx_ref[0:WINDOW, :] = vx_ref[tm:tm + WINDOW, :]
        gx_ref[0:CONV_HALO, :] = gx_ref[tm:tm + CONV_HALO, :]
        px_ref[0:POOL_HALO, :] = px_ref[tm:tm + POOL_HALO, :]

    pending = {}

    def run_head(blk):
        pending[blk] = head(blk)

    def run_tail(blk):
        tail(blk, *pending.pop(blk))

    blocks = range(tm // Q_BLK)
    return ([functools.partial(run_head, blk) for blk in blocks],
            [functools.partial(run_tail, blk) for blk in blocks], finish)


def _mixer_weight_specs(l):
    return [pl.BlockSpec(memory_space=pltpu.SMEM),
            _layer_spec(l, (CONV_K, CONV_DIM)), _layer_spec(l, (1, CONV_DIM)),
            _layer_spec(l, (1, CONV_DIM)), _layer_spec(l, (1, CONV_DIM)),
            _layer_spec(l, (CONV_DIM, CONV_DIM)), _layer_spec(l, (POOL_DIM, POOL_DIM)),
            _layer_spec(l, (1, POOL_DIM))]


PROMPT_TILES = BATCH * SEQ // TM_MIX
TILES_PER_SEQ = SEQ // TM_MIX


def _prompt_mix_out_ffn_kernel(sinks_ref, z_ref, h_ref, dww_ref, dwb_ref, lng_ref, lnb_ref, pw_ref, poolw_ref,
                               pools_ref, ng_ref, wout_ref, wg_ref, wu_ref, wd_ref,
                               y_ref, nk_ref, nv_ref, nc_ref, np_ref,
                               kx_ref, vx_ref, gx_ref, px_ref, mix_ref, h2_ref, hn_ref, *, layer):
    s = pl.program_id(0)
    i = lax.rem(jnp.minimum(s, PROMPT_TILES - 1), TILES_PER_SEQ)

    @pl.when(i == 0)
    def _():
        kx_ref[0:WINDOW, :] = jnp.zeros((WINDOW, KV_DIM), BF16)
        vx_ref[0:WINDOW, :] = jnp.zeros((WINDOW, KV_DIM), BF16)
        gx_ref[0:CONV_HALO, :] = jnp.zeros((CONV_HALO, CONV_DIM), F32)
        px_ref[0:POOL_HALO, :] = jnp.zeros((POOL_HALO, POOL_DIM), F32)

    def step(with_mixer, with_ffn):
        heads, tails, finish = [], [], None
        if with_mixer:
            heads, tails, finish = _prompt_mixer_pieces(
                i, sinks_ref, z_ref, dww_ref, dwb_ref, lng_ref, lnb_ref, pw_ref, poolw_ref, pools_ref,
                mix_ref, nk_ref, nv_ref, nc_ref, np_ref, kx_ref, vx_ref, gx_ref, px_ref, layer)
        if with_ffn:
            ffn = _swiglu(hn_ref[...], wg_ref, wu_ref, wd_ref, FF_SPLITS_FUSED, [None] + heads, tails)
            y_ref[...] = h2_ref[...] + 0.5 * _rms(ffn, ng_ref[5:6, :])
        else:
            for head, tail in zip(heads, tails):
                head()
                tail()
        if with_mixer:
            finish()
            m = jnp.dot(mix_ref[...], wout_ref[...], preferred_element_type=F32)
            h2 = h_ref[...] + _rms(m, ng_ref[3:4, :])
            h2_ref[...] = h2
            hn_ref[...] = _rms(h2, ng_ref[4:5, :]).astype(BF16)

    pl.when(s == 0)(functools.partial(step, True, False))
    pl.when((s > 0) & (s < PROMPT_TILES))(functools.partial(step, True, True))
    pl.when(s == PROMPT_TILES)(functools.partial(step, False, True))


def _prompt_mix_out_ffn(l, z, h, sinks, dww, dwb, lng, lnb, pw, poolw, pools, ng, wout, wg, wu, wd):
    last = PROMPT_TILES - 1
    cur = lambda s: (jnp.minimum(s, last), 0)
    prev = lambda s: (jnp.maximum(s - 1, 0), 0)
    seq3 = lambda s: (jnp.minimum(s, last) // TILES_PER_SEQ, 0, 0)
    wspecs = _mixer_weight_specs(l)
    return pl.pallas_call(
        functools.partial(_prompt_mix_out_ffn_kernel, layer=l),
        grid=(PROMPT_TILES + 1,),
        in_specs=[wspecs[0], pl.BlockSpec((TM_MIX, IN_DIM), cur), pl.BlockSpec((TM_MIX, D_MODEL), cur)]
                 + wspecs[1:]
                 + [_layer_spec(l, (6, D_MODEL), True), _layer_spec(l, (MIX_DIM, D_MODEL), True),
                    _layer_spec(l, (D_MODEL, D_FF), True), _layer_spec(l, (D_MODEL, D_FF), True),
                    _layer_spec(l, (D_FF, D_MODEL), True)],
        out_specs=[pl.BlockSpec((TM_MIX, D_MODEL), prev),
                   pl.BlockSpec((None, WINDOW, KV_DIM), seq3), pl.BlockSpec((None, WINDOW, KV_DIM), seq3),
                   pl.BlockSpec((None, CONV_K - 1, CONV_DIM), seq3),
                   pl.BlockSpec((None, POOL_MAX - 1, POOL_DIM), seq3)],
        out_shape=[jax.ShapeDtypeStruct((BATCH * SEQ, D_MODEL), F32),
                   jax.ShapeDtypeStruct((BATCH, WINDOW, KV_DIM), F32),
                   jax.ShapeDtypeStruct((BATCH, WINDOW, KV_DIM), F32),
                   jax.ShapeDtypeStruct((BATCH, CONV_K - 1, CONV_DIM), F32),
                   jax.ShapeDtypeStruct((BATCH, POOL_MAX - 1, POOL_DIM), F32)],
        scratch_shapes=[pltpu.VMEM((WINDOW + TM_MIX, KV_DIM), BF16),
                        pltpu.VMEM((WINDOW + TM_MIX, KV_DIM), BF16),
                        pltpu.VMEM((CONV_HALO + TM_MIX, CONV_DIM), F32),
                        pltpu.VMEM((POOL_HALO + TM_MIX, POOL_DIM), F32),
                        pltpu.VMEM((TM_MIX, MIX_DIM), BF16),
                        pltpu.VMEM((TM_MIX, D_MODEL), F32),
                        pltpu.VMEM((TM_MIX, D_MODEL), BF16)],
        compiler_params=pltpu.CompilerParams(dimension_semantics=("arbitrary",),
                                             vmem_limit_bytes=VMEM_LIMIT_FUSED),
        name="prompt_mix_out_ffn",
    )(sinks, z, h, dww, dwb, lng, lnb, pw, poolw, pools, ng, wout, wg, wu, wd)


def _sample_mixer_kernel(sinks_ref, z3_ref, z2_ref, sk_ref, sv_ref, sc_ref, sp_ref,
                         dww_ref, dwb_ref, lng_ref, lnb_ref, pw_ref, poolw_ref, pools_ref,
                         ck_ref, cv_ref, cc_ref, cp_ref,
                         mix_ref, nk_ref, nv_ref, nc_ref, np_ref,
                         kx_ref, vx_ref, gx_ref, px_ref, re_ref, *, layer):
    del ck_ref, cv_ref, cc_ref, cp_ref
    nb, t = SEQ_BLK, DEC_SEQ
    lane = lax.broadcasted_iota(jnp.int32, (KV_DIM, LANES), 1)

    for x_ref, s_ref, n_ref, off in ((kx_ref, sk_ref, nk_ref, K_OFF), (vx_ref, sv_ref, nv_ref, V_OFF)):
        new_t = z2_ref[:, off:off + KV_DIM].T
        for b in range(nb):
            old = jnp.concatenate([s_ref[b, kv] for kv in range(N_KV_HEADS)], axis=0)
            x_ref[b, :, 0:WINDOW] = old.astype(BF16)
            fresh = pltpu.roll(new_t, (LANES - b * t) % LANES, 1)
            x_ref[b, :, WINDOW:] = jnp.where(lane < t, fresh, 0.0).astype(BF16)
            shifted = jnp.where(lane < WINDOW - t, pltpu.roll(old, WINDOW - t, 1),
                                pltpu.roll(new_t, (WINDOW - t - b * t) % LANES, 1))
            for kv in range(N_KV_HEADS):
                n_ref[b, kv] = shifted[kv * HEAD_DIM:(kv + 1) * HEAD_DIM, :]

    q_io = lax.broadcasted_iota(jnp.int32, (t, S_KEYS), 0)
    s_io = lax.broadcasted_iota(jnp.int32, (t, S_KEYS), 1)
    dist = q_io + WINDOW - s_io
    valid = (dist >= 0) & (dist < WINDOW) & (s_io + (PAST_LEN - WINDOW) >= 0)
    bias = _alibi_bias(dist.astype(F32))
    sinks = [sinks_ref[layer, h] * LOG2E for h in range(N_HEADS)]

    qs = _stack_q_heads(z3_ref[:, :, Q_OFF:Q_OFF + ATTN_DIM])
    s = jnp.einsum('bqd,bdk->bqk', qs, kx_ref[...], preferred_element_type=F32)
    es, rinv = _softmax_heads(s, valid, bias, sinks, t)
    o = jnp.einsum('bqk,bdk->bqd', jnp.concatenate(es, axis=1), vx_ref[...], preferred_element_type=F32)
    o_heads = [o[:, h * t:(h + 1) * t, :] * rinv[h] for h in range(N_HEADS)]
    attn = _unstack_heads(o_heads)
    mix_ref[:, 0:ATTN_DIM] = attn.reshape(nb * t, ATTN_DIM)

    def to_token_major(x, dst_ref, first):
        for cb in range(x.shape[1] // LANES):
            re_ref[cb] = x[:, cb * LANES:(cb + 1) * LANES]
        for tok in range(t):
            for cb in range(x.shape[1] // LANES):
                dst_ref[first + tok, :, cb * LANES:(cb + 1) * LANES] = re_ref[cb, pl.ds(tok, nb, stride=t), :]

    def to_seq_major(x):
        for tok in range(t):
            for cb in range(x.shape[1] // LANES):
                re_ref[cb, pl.ds(tok, nb, stride=t), :] = x[tok * nb:(tok + 1) * nb, cb * LANES:(cb + 1) * LANES]
        return jnp.concatenate([re_ref[cb] for cb in range(x.shape[1] // LANES)], axis=1)

    hist = CONV_K - 1
    gx_ref[0:hist] = sc_ref[...]
    to_token_major(z2_ref[:, GA_OFF:GA_OFF + CONV_DIM] * jax.nn.sigmoid(z2_ref[:, GB_OFF:GB_OFF + CONV_DIM]),
                   gx_ref, hist)
    nc_ref[...] = gx_ref[t:t + hist]
    ys = []
    for tok in range(t):
        c = jnp.broadcast_to(dwb_ref[...], (nb, CONV_DIM))
        for j in range(CONV_K):
            c = c + dww_ref[j:j + 1, :] * gx_ref[tok + j]
        ys.append(_conv_ln_act(c, lng_ref, lnb_ref))
    conv = jnp.dot(jnp.concatenate(ys, axis=0), pw_ref[...], preferred_element_type=F32)
    mix_ref[:, CONV_OUT_OFF:CONV_OUT_OFF + CONV_DIM] = to_seq_major(conv)

    hist = POOL_MAX - 1
    px_ref[0:hist] = sp_ref[...]
    to_token_major(z2_ref[:, PU_OFF:PU_OFF + POOL_DIM], px_ref, hist)
    np_ref[...] = px_ref[t:t + hist]
    w_lane = _pool_window_lane()
    ds = []
    for tok in range(t):
        cur = px_ref[hist + tok]
        run = cur
        sums = []
        for back in range(1, POOL_MAX):
            run = run + px_ref[hist + tok - back]
            if back + 1 in POOL_WINDOWS:
                sums.append(run)
        cnt = jnp.minimum(w_lane, PAST_LEN + tok + 1).astype(F32)
        ds.append(_pool_diff(sums, cnt, cur))
    pool = jnp.dot(jnp.concatenate(ds, axis=0), poolw_ref[...], preferred_element_type=F32) * pools_ref[...]
    mix_ref[:, POOL_OUT_OFF:POOL_OUT_OFF + POOL_DIM] = to_seq_major(pool)


def _sample_mixer(l, z, sk, sv, sc, sp, new_states, sinks, dww, dwb, lng, lnb, pw, poolw, pools):
    nb = SEQ_BLK
    assert nb * DEC_SEQ == LANES and S_KEYS == 2 * LANES and WINDOW == LANES
    kv_spec = pl.BlockSpec((None, nb, N_KV_HEADS, HEAD_DIM, WINDOW), lambda j: (l, j, 0, 0, 0))
    rows_spec = lambda rows, ch: pl.BlockSpec((None, rows, nb, ch), lambda j: (l, 0, j, 0))
    wspecs = _mixer_weight_specs(l)
    first = BATCH * SEQ // (nb * DEC_SEQ)
    operands = (sinks, z.reshape(-1, DEC_SEQ, IN_DIM), z, sk, sv, sc, sp, dww, dwb, lng, lnb, pw, poolw, pools)
    carried = tuple(new_states)
    return pl.pallas_call(
        functools.partial(_sample_mixer_kernel, layer=l),
        grid=(DEC_BATCH // nb,),
        in_specs=[wspecs[0],
                  pl.BlockSpec((nb, DEC_SEQ, IN_DIM), lambda j: (first + j, 0, 0)),
                  pl.BlockSpec((nb * DEC_SEQ, IN_DIM), lambda j: (first + j, 0)),
                  kv_spec, kv_spec, rows_spec(CONV_K - 1, CONV_DIM), rows_spec(POOL_MAX - 1, POOL_DIM)]
                 + wspecs[1:] + [pl.BlockSpec(memory_space=pl.ANY)] * len(carried),
        out_specs=[pl.BlockSpec((nb * DEC_SEQ, MIX_DIM), lambda j: (j, 0)),
                   kv_spec, kv_spec, rows_spec(CONV_K - 1, CONV_DIM), rows_spec(POOL_MAX - 1, POOL_DIM)],
        out_shape=[jax.ShapeDtypeStruct((DEC_BATCH * DEC_SEQ, MIX_DIM), F32),
                   jax.ShapeDtypeStruct(sk.shape, F32), jax.ShapeDtypeStruct(sv.shape, F32),
                   jax.ShapeDtypeStruct(sc.shape, F32), jax.ShapeDtypeStruct(sp.shape, F32)],
        input_output_aliases={len(operands) + n: 1 + n for n in range(len(carried))},
        scratch_shapes=[pltpu.VMEM((nb, KV_DIM, S_KEYS), BF16),
                        pltpu.VMEM((nb, KV_DIM, S_KEYS), BF16),
                        pltpu.VMEM((CONV_K - 1 + DEC_SEQ, nb, CONV_DIM), F32),
                        pltpu.VMEM((POOL_MAX - 1 + DEC_SEQ, nb, POOL_DIM), F32),
                        pltpu.VMEM((max(CONV_DIM, POOL_DIM) // LANES, nb * DEC_SEQ, LANES), F32)],
        compiler_params=pltpu.CompilerParams(dimension_semantics=("arbitrary",),
                                             vmem_limit_bytes=VMEM_LIMIT),
        name="sample_mixer",
    )(*operands, *carried)


def _block_diag(pool_w):
    eye = jnp.eye(N_POOL_GROUPS, dtype=pool_w.dtype)
    return jnp.einsum('lgcd,gh->lgchd', pool_w, eye).reshape(DEPTH, POOL_DIM, POOL_DIM)


def kernel(x_prompt, x_sample, state_attn_k, state_attn_v, state_conv, state_pool, norm_g, ffn1_wg, ffn1_wu, ffn1_wd, w_in, attn_sinks, conv_dw_w, conv_dw_b, conv_ln_g, conv_ln_b, conv_pw_w, pool_w, pool_scale, w_out, ffn2_wg, ffn2_wu, ffn2_wd):
    hp = x_prompt.reshape(BATCH * SEQ, D_MODEL)
    hs = x_sample.reshape(DEC_BATCH * DEC_SEQ, D_MODEL)
    sk_all = state_attn_k.transpose(0, 1, 3, 4, 2)
    sv_all = state_attn_v.transpose(0, 1, 3, 4, 2)
    sc_all = state_conv.transpose(0, 2, 1, 3)
    sp_all = state_pool.transpose(0, 2, 1, 3)
    f1 = (ffn1_wg.astype(BF16), ffn1_wu.astype(BF16), ffn1_wd.astype(BF16))
    f2 = (ffn2_wg.astype(BF16), ffn2_wu.astype(BF16), ffn2_wd.astype(BF16))
    win = w_in.astype(BF16)
    wout = w_out.astype(BF16)
    mixer_w = (attn_sinks, conv_dw_w, conv_dw_b.reshape(DEPTH, 1, CONV_DIM),
               conv_ln_g.reshape(DEPTH, 1, CONV_DIM), conv_ln_b.reshape(DEPTH, 1, CONV_DIM),
               conv_pw_w.astype(BF16), _block_diag(pool_w).astype(BF16),
               pool_scale.reshape(DEPTH, 1, POOL_DIM))
    outs = [[] for _ in range(4)]
    sample_states = [jnp.zeros_like(a) for a in (sk_all, sv_all, sc_all, sp_all)]
    for l in range(DEPTH):
        h1, z = _ffn_in(l, hp, hs, norm_g, *f1, win)
        hp, k1, v1, c1, p1 = _prompt_mix_out_ffn(l, z, h1, *mixer_w, norm_g, wout, *f2)
        mixs, *sample_states = _sample_mixer(l, z, sk_all, sv_all, sc_all, sp_all, sample_states, *mixer_w)
        hs = _sample_out_ffn(l, h1, mixs, norm_g, wout, *f2)
        for lst, val in zip(outs, (k1, v1, c1, p1)):
            lst.append(val)

    kv_p = (DEPTH, BATCH, WINDOW, N_KV_HEADS, HEAD_DIM)
    k2, v2, c2, p2 = sample_states
    return (hp.reshape(BATCH, SEQ, D_MODEL), hs.reshape(DEC_BATCH, DEC_SEQ, D_MODEL),
            jnp.stack(outs[0]).reshape(kv_p), jnp.stack(outs[1]).reshape(kv_p),
            jnp.stack(outs[2]), jnp.stack(outs[3]),
            k2.transpose(0, 1, 4, 2, 3), v2.transpose(0, 1, 4, 2, 3),
            c2.transpose(0, 2, 1, 3), p2.transpose(0, 2, 1, 3))
```

```python
import functools

import numpy as np
import jax
import jax.numpy as jnp
from jax import lax
from jax.experimental import pallas as pl
from jax.experimental.pallas import tpu as pltpu

D_MODEL = 1024
BATCH = 8
SEQ = 2048
DEPTH = 4
DEC_BATCH = 128
DEC_SEQ = 8
PAST_LEN = 8192

N_HEADS = 8
HEAD_DIM = 64
N_KV_HEADS = 2
GROUP = N_HEADS // N_KV_HEADS
ATTN_DIM = N_HEADS * HEAD_DIM
KV_DIM = N_KV_HEADS * HEAD_DIM
WINDOW = 128
CONV_DIM = D_MODEL // 4
CONV_K = 31
POOL_DIM = D_MODEL // 4
POOL_WINDOWS = (2, 4, 8, 16)
N_POOL_GROUPS = 4
POOL_GROUP_DIM = POOL_DIM // N_POOL_GROUPS
POOL_MAX = 16
IN_DIM = ATTN_DIM + 2 * KV_DIM + 2 * CONV_DIM + POOL_DIM
MIX_DIM = ATTN_DIM + CONV_DIM + POOL_DIM
D_FF = ((8 * D_MODEL // 3 + 127) // 128) * 128
RMS_EPS = 1e-6
LN_EPS = 1e-5
NEG_INF = -1e30

Q_OFF = 0
K_OFF = ATTN_DIM
V_OFF = K_OFF + KV_DIM
GA_OFF = V_OFF + KV_DIM
GB_OFF = GA_OFF + CONV_DIM
PU_OFF = GB_OFF + CONV_DIM
CONV_OUT_OFF = ATTN_DIM
POOL_OUT_OFF = ATTN_DIM + CONV_DIM

LANES = 128
SUBLANES = 8
TM_FFN = 512
FFN_SUBTILES = 2
CAST_ROWS = 128
MXU_TILE = 256
FF_SPLITS = (4 * MXU_TILE, 4 * MXU_TILE, 3 * MXU_TILE)
FF_SPLITS_FUSED = (2 * MXU_TILE, 2 * MXU_TILE, 2 * MXU_TILE, 2 * MXU_TILE, 3 * MXU_TILE)
TM_MIX = 512
Q_BLK = 128
CONV_ROWS = 64
CONV_HALO = 32
POOL_HALO = 16
SEQ_BLK = 16
S_KEYS = 256
VMEM_LIMIT = 48 * 1024 * 1024
VMEM_LIMIT_FUSED = 54 * 1024 * 1024

LOG2E = float(np.log2(np.e))
ALIBI_SLOPES2 = [float(s) * LOG2E
                 for s in np.exp2(-8.0 * np.arange(1, N_HEADS + 1, dtype=np.float32) / N_HEADS)]
Q_SCALE2 = HEAD_DIM ** -0.5 * LOG2E

F32 = jnp.float32
BF16 = jnp.bfloat16


def _rms(x, g):
    return x * lax.rsqrt(jnp.mean(x * x, axis=-1, keepdims=True) + RMS_EPS) * g


def _swiglu(xn, wg_ref, wu_ref, wd_ref, splits=(D_FF,), before=(), after=()):
    bounds = np.cumsum((0,) + tuple(splits))
    acc = None
    act = None
    for c in range(len(splits) + 1):
        if c < len(splits):
            if c < len(before) and before[c] is not None:
                before[c]()
            sl = slice(int(bounds[c]), int(bounds[c + 1]))
            g = jnp.dot(xn, wg_ref[:, sl], preferred_element_type=F32)
            u = jnp.dot(xn, wu_ref[:, sl], preferred_element_type=F32)
            nxt = (g * jax.nn.sigmoid(g) * u).astype(BF16)
        if c > 0:
            sl = slice(int(bounds[c - 1]), int(bounds[c]))
            part = jnp.dot(act, wd_ref[sl, :], preferred_element_type=F32)
            acc = part if acc is None else acc + part
            if c - 1 < len(after) and after[c - 1] is not None:
                after[c - 1]()
        act = nxt
    return acc


def _load_weights_as_bf16(layer, pairs, stage_ref, sem_ref):
    chunks = [(src, dst, r0) for src, dst in pairs for r0 in range(0, dst.shape[0], CAST_ROWS)]

    def chunk_copy(idx):
        src, dst, r0 = chunks[idx]
        return pltpu.make_async_copy(src.at[layer, pl.ds(r0, CAST_ROWS), :],
                                     stage_ref.at[idx % 2, :, 0:dst.shape[1]], sem_ref.at[idx % 2])

    chunk_copy(0).start()
    for idx, (src, dst, r0) in enumerate(chunks):
        if idx + 1 < len(chunks):
            chunk_copy(idx + 1).start()
        chunk_copy(idx).wait()
        dst[r0:r0 + CAST_ROWS, :] = stage_ref[idx % 2, :, 0:dst.shape[1]].astype(BF16)


def _ffn_in_kernel(xp_ref, xs_ref, ng_ref, wg_hbm, wu_hbm, wd_hbm, win_hbm, h_ref, z_ref,
                   wg_ref, wu_ref, wd_ref, win_ref, stage_ref, sem_ref, *, layer):
    @pl.when(pl.program_id(0) == 0)
    def _():
        _load_weights_as_bf16(layer, ((wg_hbm, wg_ref), (wu_hbm, wu_ref), (wd_hbm, wd_ref), (win_hbm, win_ref)),
                              stage_ref, sem_ref)

    from_prompt = pl.program_id(0) < PROMPT_FFN_TILES
    sub = xp_ref.shape[0] // FFN_SUBTILES
    rows = [slice(k * sub, (k + 1) * sub) for k in range(FFN_SUBTILES)]
    xs = [jnp.where(from_prompt, xp_ref[r, :], xs_ref[r, :]) for r in rows]
    xns = [_rms(x, ng_ref[0:1, :]).astype(BF16) for x in xs]
    us = []
    for k in range(FFN_SUBTILES):
        ffn = _swiglu(xns[k], wg_ref, wu_ref, wd_ref, FF_SPLITS)
        h = xs[k] + 0.5 * _rms(ffn, ng_ref[1:2, :])
        h_ref[rows[k], :] = h
        us.append(_rms(h, ng_ref[2:3, :]).astype(BF16))
        if k > 0:
            z_ref[rows[k - 1], :] = jnp.dot(us[k - 1], win_ref[...], preferred_element_type=F32)
    z_ref[rows[-1], :] = jnp.dot(us[-1], win_ref[...], preferred_element_type=F32)


def _out_ffn_kernel(h_ref, mix_ref, ng_ref, wout_ref, wg_ref, wu_ref, wd_ref, y_ref,
                    splits=(D_FF,), before=(), after=()):
    m = jnp.dot(mix_ref[...].astype(BF16), wout_ref[...], preferred_element_type=F32)
    h = h_ref[...] + _rms(m, ng_ref[3:4, :])
    hn = _rms(h, ng_ref[4:5, :]).astype(BF16)
    y_ref[...] = h + 0.5 * _rms(_swiglu(hn, wg_ref, wu_ref, wd_ref, splits, before, after), ng_ref[5:6, :])


def _const_spec(shape):
    return pl.BlockSpec(shape, lambda *_: (0,) * len(shape), pipeline_mode=pl.Buffered(1))


def _layer_spec(l, shape, single_buffer=False):
    index_map = lambda *_: (l,) + (0,) * len(shape)
    if single_buffer:
        return pl.BlockSpec((None,) + shape, index_map, pipeline_mode=pl.Buffered(1))
    return pl.BlockSpec((None,) + shape, index_map)


PROMPT_FFN_TILES = BATCH * SEQ // TM_FFN
SAMPLE_FFN_TILES = DEC_BATCH * DEC_SEQ // TM_FFN


def _ffn_in(l, xp, xs, ng, wg, wu, wd, win):
    n = xp.shape[0] + xs.shape[0]
    row = lambda i: (i, 0)
    return pl.pallas_call(
        functools.partial(_ffn_in_kernel, layer=l),
        grid=(PROMPT_FFN_TILES + SAMPLE_FFN_TILES,),
        in_specs=[pl.BlockSpec((TM_FFN, D_MODEL), lambda i: (jnp.minimum(i, PROMPT_FFN_TILES - 1), 0)),
                  pl.BlockSpec((TM_FFN, D_MODEL), lambda i: (jnp.maximum(i - PROMPT_FFN_TILES, 0), 0)),
                  _layer_spec(l, (6, D_MODEL), True)] + [pl.BlockSpec(memory_space=pl.ANY)] * 4,
        out_specs=[pl.BlockSpec((TM_FFN, D_MODEL), row), pl.BlockSpec((TM_FFN, IN_DIM), row)],
        out_shape=[jax.ShapeDtypeStruct((n, D_MODEL), F32), jax.ShapeDtypeStruct((n, IN_DIM), F32)],
        scratch_shapes=[pltpu.VMEM((D_MODEL, D_FF), BF16), pltpu.VMEM((D_MODEL, D_FF), BF16),
                        pltpu.VMEM((D_FF, D_MODEL), BF16), pltpu.VMEM((D_MODEL, IN_DIM), BF16),
                        pltpu.VMEM((2, CAST_ROWS, D_FF), F32), pltpu.SemaphoreType.DMA((2,))],
        compiler_params=pltpu.CompilerParams(dimension_semantics=("arbitrary",),
                                             vmem_limit_bytes=VMEM_LIMIT),
        name="ffn_in",
    )(xp, xs, ng, wg, wu, wd, win)


def _sample_out_ffn(l, h, mix, ng, wout, wg, wu, wd):
    n = mix.shape[0]
    row = lambda i: (i, 0)
    return pl.pallas_call(
        _out_ffn_kernel,
        grid=(n // TM_FFN,),
        in_specs=[pl.BlockSpec((TM_FFN, D_MODEL), lambda i: (PROMPT_FFN_TILES + i, 0)),
                  pl.BlockSpec((TM_FFN, MIX_DIM), row),
                  _layer_spec(l, (6, D_MODEL), True), _layer_spec(l, (MIX_DIM, D_MODEL), True),
                  _layer_spec(l, (D_MODEL, D_FF), True), _layer_spec(l, (D_MODEL, D_FF), True),
                  _layer_spec(l, (D_FF, D_MODEL), True)],
        out_specs=pl.BlockSpec((TM_FFN, D_MODEL), row),
        out_shape=jax.ShapeDtypeStruct((n, D_MODEL), F32),
        compiler_params=pltpu.CompilerParams(dimension_semantics=("arbitrary",),
                                             vmem_limit_bytes=VMEM_LIMIT),
        name="out_ffn",
    )(h, mix, ng, wout, wg, wu, wd)


def _stack_q_heads(q):
    lane = lax.broadcasted_iota(jnp.int32, q.shape[:-1] + (LANES,), q.ndim - 1)
    pieces = []
    for h in range(N_HEADS):
        kvh = h // GROUP
        slab = q[..., (h // 2) * LANES:(h // 2 + 1) * LANES]
        if h % 2 != kvh:
            slab = pltpu.roll(slab, HEAD_DIM, q.ndim - 1)
        keep = (lane >= kvh * HEAD_DIM) & (lane < (kvh + 1) * HEAD_DIM)
        pieces.append((jnp.where(keep, slab, 0.0) * Q_SCALE2).astype(BF16))
    return jnp.concatenate(pieces, axis=-2)


def _unstack_heads(o_heads):
    ndim = o_heads[0].ndim
    lane = lax.broadcasted_iota(jnp.int32, o_heads[0].shape, ndim - 1)
    slabs = []
    for j in range(N_HEADS // 2):
        a, b = o_heads[2 * j], o_heads[2 * j + 1]
        kvh = (2 * j) // GROUP
        if kvh == 0:
            b = pltpu.roll(b, HEAD_DIM, ndim - 1)
        else:
            a = pltpu.roll(a, HEAD_DIM, ndim - 1)
        slabs.append(jnp.where(lane < HEAD_DIM, a, b))
    return jnp.concatenate(slabs, axis=-1)


def _alibi_bias(dist_f):
    return [ALIBI_SLOPES2[h] * dist_f for h in range(N_HEADS)]


def _softmax_heads(s, valid, bias, sinks, rows):
    es, rinv = [], []
    for h in range(N_HEADS):
        sh = s[..., h * rows:(h + 1) * rows, :]
        l = jnp.where(valid, sh - bias[h], NEG_INF)
        m = jnp.maximum(jnp.max(l, axis=-1, keepdims=True), sinks[h])
        e = jnp.exp2(l - m)
        den = jnp.sum(e, axis=-1, keepdims=True) + jnp.exp2(sinks[h] - m)
        es.append(e.astype(BF16))
        rinv.append(1.0 / den)
    return es, rinv


def _conv_ln_act(c, lng_ref, lnb_ref):
    mu = jnp.mean(c, axis=-1, keepdims=True)
    xc = c - mu
    y = xc * lax.rsqrt(jnp.mean(xc * xc, axis=-1, keepdims=True) + LN_EPS) * lng_ref[...] + lnb_ref[...]
    return (y * jax.nn.sigmoid(y)).astype(BF16)


def _conv_tail(c, lng_ref, lnb_ref, pw_ref):
    return jnp.dot(_conv_ln_act(c, lng_ref, lnb_ref), pw_ref[...], preferred_element_type=F32)


def _pool_diff(sums, cnt, cur):
    lane = lax.broadcasted_iota(jnp.int32, cur.shape, cur.ndim - 1)
    wsum = sums[-1]
    for g in range(N_POOL_GROUPS - 2, -1, -1):
        wsum = jnp.where(lane < (g + 1) * POOL_GROUP_DIM, sums[g], wsum)
    return (wsum / cnt - cur).astype(BF16)


def _pool_tail(sums, cnt, cur, poolw_ref, pools_ref):
    return jnp.dot(_pool_diff(sums, cnt, cur), poolw_ref[...], preferred_element_type=F32) * pools_ref[...]


def _pool_window_lane():
    lane = lax.broadcasted_iota(jnp.int32, (1, POOL_DIM), 1)
    w = jnp.full((1, POOL_DIM), POOL_WINDOWS[-1], jnp.int32)
    for g in range(N_POOL_GROUPS - 2, -1, -1):
        w = jnp.where(lane < (g + 1) * POOL_GROUP_DIM, POOL_WINDOWS[g], w)
    return w


def _dw_conv_rows(gx_ref, dww_ref, r0, rows):
    first = CONV_HALO - (CONV_K - 1)
    outs = []
    for c0 in range(0, CONV_DIM, LANES):
        cols = slice(c0, c0 + LANES)
        out = None
        for r in range(SUBLANES):
            ext = rows if r == 0 else rows + SUBLANES
            acc = None
            for m in range((first + CONV_K - 1) // SUBLANES + 1):
                j = SUBLANES * m + r - first
                if j < 0 or j >= CONV_K:
                    continue
                lo = r0 + SUBLANES * m
                term = dww_ref[j:j + 1, cols] * gx_ref[lo:lo + ext, cols]
                acc = term if acc is None else acc + term
            part = acc if r == 0 else acc[r:r + rows, :]
            out = part if out is None else out + part
        outs.append(out)
    return jnp.concatenate(outs, axis=1)


def _prompt_mixer_pieces(i, sinks_ref, z_ref, dww_ref, dwb_ref, lng_ref, lnb_ref, pw_ref, poolw_ref, pools_ref,
                         mix_ref, nk_ref, nv_ref, nc_ref, np_ref,
                         kx_ref, vx_ref, gx_ref, px_ref, layer):
    tm = TM_MIX

    kx_ref[WINDOW:, :] = z_ref[:, K_OFF:K_OFF + KV_DIM].astype(BF16)
    vx_ref[WINDOW:, :] = z_ref[:, V_OFF:V_OFF + KV_DIM].astype(BF16)
    gx_ref[CONV_HALO:, :] = z_ref[:, GA_OFF:GA_OFF + CONV_DIM] * jax.nn.sigmoid(z_ref[:, GB_OFF:GB_OFF + CONV_DIM])
    px_ref[POOL_HALO:, :] = z_ref[:, PU_OFF:PU_OFF + POOL_DIM]

    q_io = lax.broadcasted_iota(jnp.int32, (Q_BLK, WINDOW + Q_BLK), 0)
    s_io = lax.broadcasted_iota(jnp.int32, (Q_BLK, WINDOW + Q_BLK), 1)
    dist = q_io + WINDOW - s_io
    band = (dist >= 0) & (dist < WINDOW)
    bias = _alibi_bias(dist.astype(F32))
    sinks = [sinks_ref[layer, h] * LOG2E for h in range(N_HEADS)]
    w_lane = _pool_window_lane()

    def head(blk):
        r0 = blk * Q_BLK
        first_key_pos = i * tm + r0 - WINDOW
        valid = band & (s_io + first_key_pos >= 0)
        qs = _stack_q_heads(z_ref[r0:r0 + Q_BLK, Q_OFF:Q_OFF + ATTN_DIM])
        kb = kx_ref[r0:r0 + WINDOW + Q_BLK, :]
        s = lax.dot_general(qs, kb, (((1,), (1,)), ((), ())), preferred_element_type=F32)
        es, rinv = _softmax_heads(s, valid, bias, sinks, Q_BLK)
        ys = []
        for rc in range(Q_BLK // CONV_ROWS):
            c = _dw_conv_rows(gx_ref, dww_ref, r0 + rc * CONV_ROWS, CONV_ROWS) + dwb_ref[...]
            ys.append(_conv_ln_act(c, lng_ref, lnb_ref))
        run = px_ref[r0:r0 + POOL_HALO + Q_BLK, :]
        cur = run[POOL_HALO:, :]
        sums = []
        span = 1
        while span < POOL_MAX:
            run = run + pltpu.roll(run, span, 0)
            span *= 2
            if span in POOL_WINDOWS:
                sums.append(run[POOL_HALO:, :])
        pos = i * tm + r0 + lax.broadcasted_iota(jnp.int32, (Q_BLK, 1), 0)
        cnt = jnp.minimum(w_lane, pos + 1).astype(F32)
        return (jnp.concatenate(es, axis=0), rinv, jnp.concatenate(ys, axis=0), _pool_diff(sums, cnt, cur))

    def tail(blk, p, rinv, y, d):
        r0 = blk * Q_BLK
        o = jnp.dot(p, vx_ref[r0:r0 + WINDOW + Q_BLK, :], preferred_element_type=F32)
        o_heads = [o[h * Q_BLK:(h + 1) * Q_BLK, :] * rinv[h] for h in range(N_HEADS)]
        mix_ref[r0:r0 + Q_BLK, 0:ATTN_DIM] = _unstack_heads(o_heads).astype(mix_ref.dtype)
        conv = jnp.dot(y, pw_ref[...], preferred_element_type=F32)
        mix_ref[r0:r0 + Q_BLK, CONV_OUT_OFF:CONV_OUT_OFF + CONV_DIM] = conv.astype(mix_ref.dtype)
        pool = jnp.dot(d, poolw_ref[...], preferred_element_type=F32) * pools_ref[...]
        mix_ref[r0:r0 + Q_BLK, POOL_OUT_OFF:POOL_OUT_OFF + POOL_DIM] = pool.astype(mix_ref.dtype)

    def finish():
        nk_ref[...] = z_ref[tm - WINDOW:, K_OFF:K_OFF + KV_DIM]
        nv_ref[...] = z_ref[tm - WINDOW:, V_OFF:V_OFF + KV_DIM]
        nc_ref[...] = gx_ref[CONV_HALO + tm - (CONV_K - 1):, :]
        np_ref[...] = px_ref[POOL_HALO + tm - (POOL_MAX - 1):, :]
        kx_ref[0:WINDOW, :] = kx_ref[tm:tm + WINDOW, :]
        vx_ref[0:WINDOW, :] = vx_ref[tm:tm + WINDOW, :]
        gx_ref[0:CONV_HALO, :] = gx_ref[tm:tm + CONV_HALO, :]
        px_ref[0:POOL_HALO, :] = px_ref[tm:tm + POOL_HALO, :]

    pending = {}

    def run_head(blk):
        pending[blk] = head(blk)

    def run_tail(blk):
        tail(blk, *pending.pop(blk))

    blocks = range(tm // Q_BLK)
    return ([functools.partial(run_head, blk) for blk in blocks],
            [functools.partial(run_tail, blk) for blk in blocks], finish)


def _mixer_weight_specs(l):
    return [pl.BlockSpec(memory_space=pltpu.SMEM),
            _layer_spec(l, (CONV_K, CONV_DIM)), _layer_spec(l, (1, CONV_DIM)),
            _layer_spec(l, (1, CONV_DIM)), _layer_spec(l, (1, CONV_DIM)),
            _layer_spec(l, (CONV_DIM, CONV_DIM)), _layer_spec(l, (POOL_DIM, POOL_DIM)),
            _layer_spec(l, (1, POOL_DIM))]


PROMPT_TILES = BATCH * SEQ // TM_MIX
TILES_PER_SEQ = SEQ // TM_MIX


def _prompt_mix_out_ffn_kernel(sinks_ref, z_ref, h_ref, dww_ref, dwb_ref, lng_ref, lnb_ref, pw_ref, poolw_ref,
                               pools_ref, ng_ref, wout_ref, wg_ref, wu_ref, wd_ref,
                               y_ref, nk_ref, nv_ref, nc_ref, np_ref,
                               kx_ref, vx_ref, gx_ref, px_ref, mix_ref, h2_ref, hn_ref, *, layer):
    s = pl.program_id(0)
    i = lax.rem(jnp.minimum(s, PROMPT_TILES - 1), TILES_PER_SEQ)

    @pl.when(i == 0)
    def _():
        kx_ref[0:WINDOW, :] = jnp.zeros((WINDOW, KV_DIM), BF16)
        vx_ref[0:WINDOW, :] = jnp.zeros((WINDOW, KV_DIM), BF16)
        gx_ref[0:CONV_HALO, :] = jnp.zeros((CONV_HALO, CONV_DIM), F32)
        px_ref[0:POOL_HALO, :] = jnp.zeros((POOL_HALO, POOL_DIM), F32)

    @pl.when(s == 0)
    def _():
        h2_ref[...] = jnp.zeros(h2_ref.shape, h2_ref.dtype)
        hn_ref[...] = jnp.zeros(hn_ref.shape, hn_ref.dtype)

    heads, tails, finish = _prompt_mixer_pieces(i, sinks_ref, z_ref, dww_ref, dwb_ref, lng_ref, lnb_ref, pw_ref,
                                                poolw_ref, pools_ref, mix_ref, nk_ref, nv_ref, nc_ref, np_ref,
                                                kx_ref, vx_ref, gx_ref, px_ref, layer)
    ffn = _swiglu(hn_ref[...], wg_ref, wu_ref, wd_ref, FF_SPLITS_FUSED, [None] + heads, tails)
    finish()
    y_ref[...] = h2_ref[...] + 0.5 * _rms(ffn, ng_ref[5:6, :])
    m = jnp.dot(mix_ref[...], wout_ref[...], preferred_element_type=F32)
    h2 = h_ref[...] + _rms(m, ng_ref[3:4, :])
    h2_ref[...] = h2
    hn_ref[...] = _rms(h2, ng_ref[4:5, :]).astype(BF16)


def _prompt_mix_out_ffn(l, z, h, sinks, dww, dwb, lng, lnb, pw, poolw, pools, ng, wout, wg, wu, wd):
    last = PROMPT_TILES - 1
    cur = lambda s: (jnp.minimum(s, last), 0)
    prev = lambda s: (jnp.maximum(s - 1, 0), 0)
    seq3 = lambda s: (jnp.minimum(s, last) // TILES_PER_SEQ, 0, 0)
    wspecs = _mixer_weight_specs(l)
    return pl.pallas_call(
        functools.partial(_prompt_mix_out_ffn_kernel, layer=l),
        grid=(PROMPT_TILES + 1,),
        in_specs=[wspecs[0], pl.BlockSpec((TM_MIX, IN_DIM), cur), pl.BlockSpec((TM_MIX, D_MODEL), cur)]
                 + wspecs[1:]
                 + [_layer_spec(l, (6, D_MODEL), True), _layer_spec(l, (MIX_DIM, D_MODEL), True),
                    _layer_spec(l, (D_MODEL, D_FF), True), _layer_spec(l, (D_MODEL, D_FF), True),
                    _layer_spec(l, (D_FF, D_MODEL), True)],
        out_specs=[pl.BlockSpec((TM_MIX, D_MODEL), prev),
                   pl.BlockSpec((None, WINDOW, KV_DIM), seq3), pl.BlockSpec((None, WINDOW, KV_DIM), seq3),
                   pl.BlockSpec((None, CONV_K - 1, CONV_DIM), seq3),
                   pl.BlockSpec((None, POOL_MAX - 1, POOL_DIM), seq3)],
        out_shape=[jax.ShapeDtypeStruct((BATCH * SEQ, D_MODEL), F32),
                   jax.ShapeDtypeStruct((BATCH, WINDOW, KV_DIM), F32),
                   jax.ShapeDtypeStruct((BATCH, WINDOW, KV_DIM), F32),
                   jax.ShapeDtypeStruct((BATCH, CONV_K - 1, CONV_DIM), F32),
                   jax.ShapeDtypeStruct((BATCH, POOL_MAX - 1, POOL_DIM), F32)],
        scratch_shapes=[pltpu.VMEM((WINDOW + TM_MIX, KV_DIM), BF16),
                        pltpu.VMEM((WINDOW + TM_MIX, KV_DIM), BF16),
                        pltpu.VMEM((CONV_HALO + TM_MIX, CONV_DIM), F32),
                        pltpu.VMEM((POOL_HALO + TM_MIX, POOL_DIM), F32),
                        pltpu.VMEM((TM_MIX, MIX_DIM), BF16),
                        pltpu.VMEM((TM_MIX, D_MODEL), F32),
                        pltpu.VMEM((TM_MIX, D_MODEL), BF16)],
        compiler_params=pltpu.CompilerParams(dimension_semantics=("arbitrary",),
                                             vmem_limit_bytes=VMEM_LIMIT_FUSED),
        name="prompt_mix_out_ffn",
    )(sinks, z, h, dww, dwb, lng, lnb, pw, poolw, pools, ng, wout, wg, wu, wd)


def _sample_mixer_kernel(sinks_ref, z3_ref, z2_ref, sk_ref, sv_ref, sc_ref, sp_ref,
                         dww_ref, dwb_ref, lng_ref, lnb_ref, pw_ref, poolw_ref, pools_ref,
                         ck_ref, cv_ref, cc_ref, cp_ref,
                         mix_ref, nk_ref, nv_ref, nc_ref, np_ref,
                         kx_ref, vx_ref, gx_ref, px_ref, re_ref, *, layer):
    del ck_ref, cv_ref, cc_ref, cp_ref
    nb, t = SEQ_BLK, DEC_SEQ
    lane = lax.broadcasted_iota(jnp.int32, (KV_DIM, LANES), 1)

    for x_ref, s_ref, n_ref, off in ((kx_ref, sk_ref, nk_ref, K_OFF), (vx_ref, sv_ref, nv_ref, V_OFF)):
        new_t = z2_ref[:, off:off + KV_DIM].T
        for b in range(nb):
            old = jnp.concatenate([s_ref[b, kv] for kv in range(N_KV_HEADS)], axis=0)
            x_ref[b, :, 0:WINDOW] = old.astype(BF16)
            fresh = pltpu.roll(new_t, (LANES - b * t) % LANES, 1)
            x_ref[b, :, WINDOW:] = jnp.where(lane < t, fresh, 0.0).astype(BF16)
            shifted = jnp.where(lane < WINDOW - t, pltpu.roll(old, WINDOW - t, 1),
                                pltpu.roll(new_t, (WINDOW - t - b * t) % LANES, 1))
            for kv in range(N_KV_HEADS):
                n_ref[b, kv] = shifted[kv * HEAD_DIM:(kv + 1) * HEAD_DIM, :]

    q_io = lax.broadcasted_iota(jnp.int32, (t, S_KEYS), 0)
    s_io = lax.broadcasted_iota(jnp.int32, (t, S_KEYS), 1)
    dist = q_io + WINDOW - s_io
    valid = (dist >= 0) & (dist < WINDOW) & (s_io + (PAST_LEN - WINDOW) >= 0)
    bias = _alibi_bias(dist.astype(F32))
    sinks = [sinks_ref[layer, h] * LOG2E for h in range(N_HEADS)]

    qs = _stack_q_heads(z3_ref[:, :, Q_OFF:Q_OFF + ATTN_DIM])
    s = jnp.einsum('bqd,bdk->bqk', qs, kx_ref[...], preferred_element_type=F32)
    es, rinv = _softmax_heads(s, valid, bias, sinks, t)
    o = jnp.einsum('bqk,bdk->bqd', jnp.concatenate(es, axis=1), vx_ref[...], preferred_element_type=F32)
    o_heads = [o[:, h * t:(h + 1) * t, :] * rinv[h] for h in range(N_HEADS)]
    attn = _unstack_heads(o_heads)
    mix_ref[:, 0:ATTN_DIM] = attn.reshape(nb * t, ATTN_DIM)

    def to_token_major(x, dst_ref, first):
        for cb in range(x.shape[1] // LANES):
            re_ref[cb] = x[:, cb * LANES:(cb + 1) * LANES]
        for tok in range(t):
            for cb in range(x.shape[1] // LANES):
                dst_ref[first + tok, :, cb * LANES:(cb + 1) * LANES] = re_ref[cb, pl.ds(tok, nb, stride=t), :]

    def to_seq_major(x):
        for tok in range(t):
            for cb in range(x.shape[1] // LANES):
                re_ref[cb, pl.ds(tok, nb, stride=t), :] = x[tok * nb:(tok + 1) * nb, cb * LANES:(cb + 1) * LANES]
        return jnp.concatenate([re_ref[cb] for cb in range(x.shape[1] // LANES)], axis=1)

    hist = CONV_K - 1
    gx_ref[0:hist] = sc_ref[...]
    to_token_major(z2_ref[:, GA_OFF:GA_OFF + CONV_DIM] * jax.nn.sigmoid(z2_ref[:, GB_OFF:GB_OFF + CONV_DIM]),
                   gx_ref, hist)
    nc_ref[...] = gx_ref[t:t + hist]
    ys = []
    for tok in range(t):
        c = jnp.broadcast_to(dwb_ref[...], (nb, CONV_DIM))
        for j in range(CONV_K):
            c = c + dww_ref[j:j + 1, :] * gx_ref[tok + j]
        ys.append(_conv_ln_act(c, lng_ref, lnb_ref))
    conv = jnp.dot(jnp.concatenate(ys, axis=0), pw_ref[...], preferred_element_type=F32)
    mix_ref[:, CONV_OUT_OFF:CONV_OUT_OFF + CONV_DIM] = to_seq_major(conv)

    hist = POOL_MAX - 1
    px_ref[0:hist] = sp_ref[...]
    to_token_major(z2_ref[:, PU_OFF:PU_OFF + POOL_DIM], px_ref, hist)
    np_ref[...] = px_ref[t:t + hist]
    w_lane = _pool_window_lane()
    ds = []
    for tok in range(t):
        cur = px_ref[hist + tok]
        run = cur
        sums = []
        for back in range(1, POOL_MAX):
            run = run + px_ref[hist + tok - back]
            if back + 1 in POOL_WINDOWS:
                sums.append(run)
        cnt = jnp.minimum(w_lane, PAST_LEN + tok + 1).astype(F32)
        ds.append(_pool_diff(sums, cnt, cur))
    pool = jnp.dot(jnp.concatenate(ds, axis=0), poolw_ref[...], preferred_element_type=F32) * pools_ref[...]
    mix_ref[:, POOL_OUT_OFF:POOL_OUT_OFF + POOL_DIM] = to_seq_major(pool)


def _sample_mixer(l, z, sk, sv, sc, sp, new_states, sinks, dww, dwb, lng, lnb, pw, poolw, pools):
    nb = SEQ_BLK
    assert nb * DEC_SEQ == LANES and S_KEYS == 2 * LANES and WINDOW == LANES
    kv_spec = pl.BlockSpec((None, nb, N_KV_HEADS, HEAD_DIM, WINDOW), lambda j: (l, j, 0, 0, 0))
    rows_spec = lambda rows, ch: pl.BlockSpec((None, rows, nb, ch), lambda j: (l, 0, j, 0))
    wspecs = _mixer_weight_specs(l)
    first = BATCH * SEQ // (nb * DEC_SEQ)
    operands = (sinks, z.reshape(-1, DEC_SEQ, IN_DIM), z, sk, sv, sc, sp, dww, dwb, lng, lnb, pw, poolw, pools)
    carried = tuple(new_states)
    return pl.pallas_call(
        functools.partial(_sample_mixer_kernel, layer=l),
        grid=(DEC_BATCH // nb,),
        in_specs=[wspecs[0],
                  pl.BlockSpec((nb, DEC_SEQ, IN_DIM), lambda j: (first + j, 0, 0)),
                  pl.BlockSpec((nb * DEC_SEQ, IN_DIM), lambda j: (first + j, 0)),
                  kv_spec, kv_spec, rows_spec(CONV_K - 1, CONV_DIM), rows_spec(POOL_MAX - 1, POOL_DIM)]
                 + wspecs[1:] + [pl.BlockSpec(memory_space=pl.ANY)] * len(carried),
        out_specs=[pl.BlockSpec((nb * DEC_SEQ, MIX_DIM), lambda j: (j, 0)),
                   kv_spec, kv_spec, rows_spec(CONV_K - 1, CONV_DIM), rows_spec(POOL_MAX - 1, POOL_DIM)],
        out_shape=[jax.ShapeDtypeStruct((DEC_BATCH * DEC_SEQ, MIX_DIM), F32),
                   jax.ShapeDtypeStruct(sk.shape, F32), jax.ShapeDtypeStruct(sv.shape, F32),
                   jax.ShapeDtypeStruct(sc.shape, F32), jax.ShapeDtypeStruct(sp.shape, F32)],
        input_output_aliases={len(operands) + n: 1 + n for n in range(len(carried))},
        scratch_shapes=[pltpu.VMEM((nb, KV_DIM, S_KEYS), BF16),
                        pltpu.VMEM((nb, KV_DIM, S_KEYS), BF16),
                        pltpu.VMEM((CONV_K - 1 + DEC_SEQ, nb, CONV_DIM), F32),
                        pltpu.VMEM((POOL_MAX - 1 + DEC_SEQ, nb, POOL_DIM), F32),
                        pltpu.VMEM((max(CONV_DIM, POOL_DIM) // LANES, nb * DEC_SEQ, LANES), F32)],
        compiler_params=pltpu.CompilerParams(dimension_semantics=("arbitrary",),
                                             vmem_limit_bytes=VMEM_LIMIT),
        name="sample_mixer",
    )(*operands, *carried)


def _block_diag(pool_w):
    eye = jnp.eye(N_POOL_GROUPS, dtype=pool_w.dtype)
    return jnp.einsum('lgcd,gh->lgchd', pool_w, eye).reshape(DEPTH, POOL_DIM, POOL_DIM)


def kernel(x_prompt, x_sample, state_attn_k, state_attn_v, state_conv, state_pool, norm_g, ffn1_wg, ffn1_wu, ffn1_wd, w_in, attn_sinks, conv_dw_w, conv_dw_b, conv_ln_g, conv_ln_b, conv_pw_w, pool_w, pool_scale, w_out, ffn2_wg, ffn2_wu, ffn2_wd):
    hp = x_prompt.reshape(BATCH * SEQ, D_MODEL)
    hs = x_sample.reshape(DEC_BATCH * DEC_SEQ, D_MODEL)
    sk_all = state_attn_k.transpose(0, 1, 3, 4, 2)
    sv_all = state_attn_v.transpose(0, 1, 3, 4, 2)
    sc_all = state_conv.transpose(0, 2, 1, 3)
    sp_all = state_pool.transpose(0, 2, 1, 3)
    f1 = (ffn1_wg, ffn1_wu, ffn1_wd)
    f2 = (ffn2_wg.astype(BF16), ffn2_wu.astype(BF16), ffn2_wd.astype(BF16))
    win = w_in
    wout = w_out.astype(BF16)
    mixer_w = (attn_sinks, conv_dw_w, conv_dw_b.reshape(DEPTH, 1, CONV_DIM),
               conv_ln_g.reshape(DEPTH, 1, CONV_DIM), conv_ln_b.reshape(DEPTH, 1, CONV_DIM),
               conv_pw_w.astype(BF16), _block_diag(pool_w).astype(BF16),
               pool_scale.reshape(DEPTH, 1, POOL_DIM))
    outs = [[] for _ in range(4)]
    sample_states = [jnp.zeros_like(a) for a in (sk_all, sv_all, sc_all, sp_all)]
    for l in range(DEPTH):
        h1, z = _ffn_in(l, hp, hs, norm_g, *f1, win)
        hp, k1, v1, c1, p1 = _prompt_mix_out_ffn(l, z, h1, *mixer_w, norm_g, wout, *f2)
        mixs, *sample_states = _sample_mixer(l, z, sk_all, sv_all, sc_all, sp_all, sample_states, *mixer_w)
        hs = _sample_out_ffn(l, h1, mixs, norm_g, wout, *f2)
        for lst, val in zip(outs, (k1, v1, c1, p1)):
            lst.append(val)

    kv_p = (DEPTH, BATCH, WINDOW, N_KV_HEADS, HEAD_DIM)
    k2, v2, c2, p2 = sample_states
    return (hp.reshape(BATCH, SEQ, D_MODEL), hs.reshape(DEC_BATCH, DEC_SEQ, D_MODEL),
            jnp.stack(outs[0]).reshape(kv_p), jnp.stack(outs[1]).reshape(kv_p),
            jnp.stack(outs[2]), jnp.stack(outs[3]),
            k2.transpose(0, 1, 4, 2, 3), v2.transpose(0, 1, 4, 2, 3),
            c2.transpose(0, 2, 1, 3), p2.transpose(0, 2, 1, 3))
```

```python
import functools

import numpy as np
import jax
import jax.numpy as jnp
from jax import lax
from jax.experimental import pallas as pl
from jax.experimental.pallas import tpu as pltpu

D_MODEL = 1024
BATCH = 8
SEQ = 2048
DEPTH = 4
DEC_BATCH = 128
DEC_SEQ = 8
PAST_LEN = 8192

N_HEADS = 8
HEAD_DIM = 64
N_KV_HEADS = 2
GROUP = N_HEADS // N_KV_HEADS
ATTN_DIM = N_HEADS * HEAD_DIM
KV_DIM = N_KV_HEADS * HEAD_DIM
WINDOW = 128
CONV_DIM = D_MODEL // 4
CONV_K = 31
POOL_DIM = D_MODEL // 4
POOL_WINDOWS = (2, 4, 8, 16)
N_POOL_GROUPS = 4
POOL_GROUP_DIM = POOL_DIM // N_POOL_GROUPS
POOL_MAX = 16
IN_DIM = ATTN_DIM + 2 * KV_DIM + 2 * CONV_DIM + POOL_DIM
MIX_DIM = ATTN_DIM + CONV_DIM + POOL_DIM
D_FF = ((8 * D_MODEL // 3 + 127) // 128) * 128
RMS_EPS = 1e-6
LN_EPS = 1e-5
NEG_INF = -1e30

Q_OFF = 0
K_OFF = ATTN_DIM
V_OFF = K_OFF + KV_DIM
GA_OFF = V_OFF + KV_DIM
GB_OFF = GA_OFF + CONV_DIM
PU_OFF = GB_OFF + CONV_DIM
CONV_OUT_OFF = ATTN_DIM
POOL_OUT_OFF = ATTN_DIM + CONV_DIM

LANES = 128
SUBLANES = 8
TM_FFN = 512
FFN_SUBTILES = 2
CAST_ROWS = 128
CAST_SLOTS = 4
MXU_TILE = 256
FF_SPLITS = (4 * MXU_TILE, 4 * MXU_TILE, 3 * MXU_TILE)
FF_SPLITS_FUSED = (1 * MXU_TILE, 3 * MXU_TILE, 2 * MXU_TILE, 2 * MXU_TILE, 3 * MXU_TILE)
TM_MIX = 512
Q_BLK = 128
CONV_ROWS = 64
CONV_HALO = 32
POOL_HALO = 16
SEQ_BLK = 16
S_KEYS = 256
VMEM_LIMIT = 48 * 1024 * 1024
VMEM_LIMIT_FUSED = 54 * 1024 * 1024

LOG2E = float(np.log2(np.e))
ALIBI_SLOPES2 = [float(s) * LOG2E
                 for s in np.exp2(-8.0 * np.arange(1, N_HEADS + 1, dtype=np.float32) / N_HEADS)]
Q_SCALE2 = HEAD_DIM ** -0.5 * LOG2E

F32 = jnp.float32
BF16 = jnp.bfloat16


def _rms(x, g):
    return x * lax.rsqrt(jnp.mean(x * x, axis=-1, keepdims=True) + RMS_EPS) * g


def _swiglu(xn, wg_ref, wu_ref, wd_ref, splits=(D_FF,), before=(), after=()):
    bounds = np.cumsum((0,) + tuple(splits))
    acc = None
    act = None
    for c in range(len(splits) + 1):
        if c < len(splits):
            if c < len(before) and before[c] is not None:
                before[c]()
            sl = slice(int(bounds[c]), int(bounds[c + 1]))
            g = jnp.dot(xn, wg_ref[:, sl], preferred_element_type=F32)
            u = jnp.dot(xn, wu_ref[:, sl], preferred_element_type=F32)
            nxt = (g * jax.nn.sigmoid(g) * u).astype(BF16)
        if c > 0:
            sl = slice(int(bounds[c - 1]), int(bounds[c]))
            part = jnp.dot(act, wd_ref[sl, :], preferred_element_type=F32)
            acc = part if acc is None else acc + part
            if c - 1 < len(after) and after[c - 1] is not None:
                after[c - 1]()
        act = nxt
    return acc


def _load_weights_as_bf16(layer, pairs, stage_ref, sem_ref):
    slots = stage_ref.shape[0]
    chunks = [(src, dst, r0) for src, dst in pairs for r0 in range(0, dst.shape[0], CAST_ROWS)]

    def chunk_copy(idx):
        src, dst, r0 = chunks[idx]
        return pltpu.make_async_copy(src.at[layer, pl.ds(r0, CAST_ROWS), :],
                                     stage_ref.at[idx % slots, :, 0:dst.shape[1]], sem_ref.at[idx % slots])

    for idx in range(slots - 1):
        chunk_copy(idx).start()
    for idx, (src, dst, r0) in enumerate(chunks):
        if idx + slots - 1 < len(chunks):
            chunk_copy(idx + slots - 1).start()
        chunk_copy(idx).wait()
        dst[r0:r0 + CAST_ROWS, :] = stage_ref[idx % slots, :, 0:dst.shape[1]].astype(BF16)


def _ffn_in_kernel(xp_ref, xs_ref, ng_ref, wg_hbm, wu_hbm, wd_hbm, win_hbm, h_ref, z_ref,
                   wg_ref, wu_ref, wd_ref, win_ref, stage_ref, sem_ref, *, layer):
    @pl.when(pl.program_id(0) == 0)
    def _():
        _load_weights_as_bf16(layer, ((wg_hbm, wg_ref), (wu_hbm, wu_ref), (wd_hbm, wd_ref), (win_hbm, win_ref)),
                              stage_ref, sem_ref)

    from_prompt = pl.program_id(0) < PROMPT_FFN_TILES
    sub = xp_ref.shape[0] // FFN_SUBTILES
    rows = [slice(k * sub, (k + 1) * sub) for k in range(FFN_SUBTILES)]
    xs = [jnp.where(from_prompt, xp_ref[r, :], xs_ref[r, :]) for r in rows]
    xns = [_rms(x, ng_ref[0:1, :]).astype(BF16) for x in xs]
    us = []
    for k in range(FFN_SUBTILES):
        ffn = _swiglu(xns[k], wg_ref, wu_ref, wd_ref, FF_SPLITS)
        h = xs[k] + 0.5 * _rms(ffn, ng_ref[1:2, :])
        h_ref[rows[k], :] = h
        us.append(_rms(h, ng_ref[2:3, :]).astype(BF16))
        if k > 0:
            z_ref[rows[k - 1], :] = jnp.dot(us[k - 1], win_ref[...], preferred_element_type=F32)
    z_ref[rows[-1], :] = jnp.dot(us[-1], win_ref[...], preferred_element_type=F32)


def _out_ffn_kernel(h_ref, mix_ref, ng_ref, wout_ref, wg_ref, wu_ref, wd_ref, y_ref,
                    splits=(D_FF,), before=(), after=()):
    m = jnp.dot(mix_ref[...].astype(BF16), wout_ref[...], preferred_element_type=F32)
    h = h_ref[...] + _rms(m, ng_ref[3:4, :])
    hn = _rms(h, ng_ref[4:5, :]).astype(BF16)
    y_ref[...] = h + 0.5 * _rms(_swiglu(hn, wg_ref, wu_ref, wd_ref, splits, before, after), ng_ref[5:6, :])


def _const_spec(shape):
    return pl.BlockSpec(shape, lambda *_: (0,) * len(shape), pipeline_mode=pl.Buffered(1))


def _layer_spec(l, shape, single_buffer=False):
    index_map = lambda *_: (l,) + (0,) * len(shape)
    if single_buffer:
        return pl.BlockSpec((None,) + shape, index_map, pipeline_mode=pl.Buffered(1))
    return pl.BlockSpec((None,) + shape, index_map)


PROMPT_FFN_TILES = BATCH * SEQ // TM_FFN
SAMPLE_FFN_TILES = DEC_BATCH * DEC_SEQ // TM_FFN


def _ffn_in(l, xp, xs, ng, wg, wu, wd, win):
    n = xp.shape[0] + xs.shape[0]
    row = lambda i: (i, 0)
    return pl.pallas_call(
        functools.partial(_ffn_in_kernel, layer=l),
        grid=(PROMPT_FFN_TILES + SAMPLE_FFN_TILES,),
        in_specs=[pl.BlockSpec((TM_FFN, D_MODEL), lambda i: (jnp.minimum(i, PROMPT_FFN_TILES - 1), 0)),
                  pl.BlockSpec((TM_FFN, D_MODEL), lambda i: (jnp.maximum(i - PROMPT_FFN_TILES, 0), 0)),
                  _layer_spec(l, (6, D_MODEL), True)] + [pl.BlockSpec(memory_space=pl.ANY)] * 4,
        out_specs=[pl.BlockSpec((TM_FFN, D_MODEL), row), pl.BlockSpec((TM_FFN, IN_DIM), row)],
        out_shape=[jax.ShapeDtypeStruct((n, D_MODEL), F32), jax.ShapeDtypeStruct((n, IN_DIM), F32)],
        scratch_shapes=[pltpu.VMEM((D_MODEL, D_FF), BF16), pltpu.VMEM((D_MODEL, D_FF), BF16),
                        pltpu.VMEM((D_FF, D_MODEL), BF16), pltpu.VMEM((D_MODEL, IN_DIM), BF16),
                        pltpu.VMEM((CAST_SLOTS, CAST_ROWS, D_FF), F32), pltpu.SemaphoreType.DMA((CAST_SLOTS,))],
        compiler_params=pltpu.CompilerParams(dimension_semantics=("arbitrary",),
                                             vmem_limit_bytes=VMEM_LIMIT),
        name="ffn_in",
    )(xp, xs, ng, wg, wu, wd, win)


def _stack_q_heads(q):
    lane = lax.broadcasted_iota(jnp.int32, q.shape[:-1] + (LANES,), q.ndim - 1)
    pieces = []
    for h in range(N_HEADS):
        kvh = h // GROUP
        slab = q[..., (h // 2) * LANES:(h // 2 + 1) * LANES]
        if h % 2 != kvh:
            slab = pltpu.roll(slab, HEAD_DIM, q.ndim - 1)
        keep = (lane >= kvh * HEAD_DIM) & (lane < (kvh + 1) * HEAD_DIM)
        pieces.append((jnp.where(keep, slab, 0.0) * Q_SCALE2).astype(BF16))
    return jnp.concatenate(pieces, axis=-2)


def _unstack_heads(o_heads):
    ndim = o_heads[0].ndim
    lane = lax.broadcasted_iota(jnp.int32, o_heads[0].shape, ndim - 1)
    slabs = []
    for j in range(N_HEADS // 2):
        a, b = o_heads[2 * j], o_heads[2 * j + 1]
        kvh = (2 * j) // GROUP
        if kvh == 0:
            b = pltpu.roll(b, HEAD_DIM, ndim - 1)
        else:
            a = pltpu.roll(a, HEAD_DIM, ndim - 1)
        slabs.append(jnp.where(lane < HEAD_DIM, a, b))
    return jnp.concatenate(slabs, axis=-1)


def _alibi_bias(dist_f):
    return [ALIBI_SLOPES2[h] * dist_f for h in range(N_HEADS)]


def _softmax_heads(s, valid, bias, sinks, rows):
    es, rinv = [], []
    for h in range(N_HEADS):
        sh = s[..., h * rows:(h + 1) * rows, :]
        l = jnp.where(valid, sh - bias[h], NEG_INF)
        m = jnp.maximum(jnp.max(l, axis=-1, keepdims=True), sinks[h])
        e = jnp.exp2(l - m)
        den = jnp.sum(e, axis=-1, keepdims=True) + jnp.exp2(sinks[h] - m)
        es.append(e.astype(BF16))
        rinv.append(1.0 / den)
    return es, rinv


def _conv_ln_act(c, lng_ref, lnb_ref):
    mu = jnp.mean(c, axis=-1, keepdims=True)
    xc = c - mu
    y = xc * lax.rsqrt(jnp.mean(xc * xc, axis=-1, keepdims=True) + LN_EPS) * lng_ref[...] + lnb_ref[...]
    return (y * jax.nn.sigmoid(y)).astype(BF16)


def _conv_tail(c, lng_ref, lnb_ref, pw_ref):
    return jnp.dot(_conv_ln_act(c, lng_ref, lnb_ref), pw_ref[...], preferred_element_type=F32)


def _pool_diff(sums, cnt, cur):
    lane = lax.broadcasted_iota(jnp.int32, cur.shape, cur.ndim - 1)
    wsum = sums[-1]
    for g in range(N_POOL_GROUPS - 2, -1, -1):
        wsum = jnp.where(lane < (g + 1) * POOL_GROUP_DIM, sums[g], wsum)
    return (wsum / cnt - cur).astype(BF16)


def _pool_tail(sums, cnt, cur, poolw_ref, pools_ref):
    return jnp.dot(_pool_diff(sums, cnt, cur), poolw_ref[...], preferred_element_type=F32) * pools_ref[...]


def _pool_window_lane():
    lane = lax.broadcasted_iota(jnp.int32, (1, POOL_DIM), 1)
    w = jnp.full((1, POOL_DIM), POOL_WINDOWS[-1], jnp.int32)
    for g in range(N_POOL_GROUPS - 2, -1, -1):
        w = jnp.where(lane < (g + 1) * POOL_GROUP_DIM, POOL_WINDOWS[g], w)
    return w


def _dw_conv_rows(gx_ref, dww_ref, r0, rows):
    first = CONV_HALO - (CONV_K - 1)
    outs = []
    for c0 in range(0, CONV_DIM, LANES):
        cols = slice(c0, c0 + LANES)
        out = None
        for r in range(SUBLANES):
            ext = rows if r == 0 else rows + SUBLANES
            acc = None
            for m in range((first + CONV_K - 1) // SUBLANES + 1):
                j = SUBLANES * m + r - first
                if j < 0 or j >= CONV_K:
                    continue
                lo = r0 + SUBLANES * m
                term = dww_ref[j:j + 1, cols] * gx_ref[lo:lo + ext, cols]
                acc = term if acc is None else acc + term
            part = acc if r == 0 else acc[r:r + rows, :]
            out = part if out is None else out + part
        outs.append(out)
    return jnp.concatenate(outs, axis=1)


def _prompt_mixer_pieces(i, sinks_ref, z_ref, dww_ref, dwb_ref, lng_ref, lnb_ref, pw_ref, poolw_ref, pools_ref,
                         mix_ref, nk_ref, nv_ref, nc_ref, np_ref,
                         kx_ref, vx_ref, gx_ref, px_ref, layer):
    tm = TM_MIX

    kx_ref[WINDOW:, :] = z_ref[:, K_OFF:K_OFF + KV_DIM].astype(BF16)
    vx_ref[WINDOW:, :] = z_ref[:, V_OFF:V_OFF + KV_DIM].astype(BF16)
    gx_ref[CONV_HALO:, :] = z_ref[:, GA_OFF:GA_OFF + CONV_DIM] * jax.nn.sigmoid(z_ref[:, GB_OFF:GB_OFF + CONV_DIM])
    px_ref[POOL_HALO:, :] = z_ref[:, PU_OFF:PU_OFF + POOL_DIM]

    q_io = lax.broadcasted_iota(jnp.int32, (Q_BLK, WINDOW + Q_BLK), 0)
    s_io = lax.broadcasted_iota(jnp.int32, (Q_BLK, WINDOW + Q_BLK), 1)
    dist = q_io + WINDOW - s_io
    band = (dist >= 0) & (dist < WINDOW)
    bias = _alibi_bias(dist.astype(F32))
    sinks = [sinks_ref[layer, h] * LOG2E for h in range(N_HEADS)]
    w_lane = _pool_window_lane()

    def head(blk):
        r0 = blk * Q_BLK
        first_key_pos = i * tm + r0 - WINDOW
        valid = band & (s_io + first_key_pos >= 0)
        qs = _stack_q_heads(z_ref[r0:r0 + Q_BLK, Q_OFF:Q_OFF + ATTN_DIM])
        kb = kx_ref[r0:r0 + WINDOW + Q_BLK, :]
        s = lax.dot_general(qs, kb, (((1,), (1,)), ((), ())), preferred_element_type=F32)
        es, rinv = _softmax_heads(s, valid, bias, sinks, Q_BLK)
        ys = []
        for rc in range(Q_BLK // CONV_ROWS):
            c = _dw_conv_rows(gx_ref, dww_ref, r0 + rc * CONV_ROWS, CONV_ROWS) + dwb_ref[...]
            ys.append(_conv_ln_act(c, lng_ref, lnb_ref))
        run = px_ref[r0:r0 + POOL_HALO + Q_BLK, :]
        cur = run[POOL_HALO:, :]
        sums = []
        span = 1
        while span < POOL_MAX:
            run = run + pltpu.roll(run, span, 0)
            span *= 2
            if span in POOL_WINDOWS:
                sums.append(run[POOL_HALO:, :])
        pos = i * tm + r0 + lax.broadcasted_iota(jnp.int32, (Q_BLK, 1), 0)
        cnt = jnp.minimum(w_lane, pos + 1).astype(F32)
        return (jnp.concatenate(es, axis=0), rinv, jnp.concatenate(ys, axis=0), _pool_diff(sums, cnt, cur))

    def tail(blk, p, rinv, y, d):
        r0 = blk * Q_BLK
        o = jnp.dot(p, vx_ref[r0:r0 + WINDOW + Q_BLK, :], preferred_element_type=F32)
        o_heads = [o[h * Q_BLK:(h + 1) * Q_BLK, :] * rinv[h] for h in range(N_HEADS)]
        mix_ref[r0:r0 + Q_BLK, 0:ATTN_DIM] = _unstack_heads(o_heads).astype(mix_ref.dtype)
        conv = jnp.dot(y, pw_ref[...], preferred_element_type=F32)
        mix_ref[r0:r0 + Q_BLK, CONV_OUT_OFF:CONV_OUT_OFF + CONV_DIM] = conv.astype(mix_ref.dtype)
        pool = jnp.dot(d, poolw_ref[...], preferred_element_type=F32) * pools_ref[...]
        mix_ref[r0:r0 + Q_BLK, POOL_OUT_OFF:POOL_OUT_OFF + POOL_DIM] = pool.astype(mix_ref.dtype)

    def finish():
        nk_ref[...] = z_ref[tm - WINDOW:, K_OFF:K_OFF + KV_DIM]
        nv_ref[...] = z_ref[tm - WINDOW:, V_OFF:V_OFF + KV_DIM]
        nc_ref[...] = gx_ref[CONV_HALO + tm - (CONV_K - 1):, :]
        np_ref[...] = px_ref[POOL_HALO + tm - (POOL_MAX - 1):, :]
        kx_ref[0:WINDOW, :] = kx_ref[tm:tm + WINDOW, :]
        vx_ref[0:WINDOW, :] = vx_ref[tm:tm + WINDOW, :]
        gx_ref[0:CONV_HALO, :] = gx_ref[tm:tm + CONV_HALO, :]
        px_ref[0:POOL_HALO, :] = px_ref[tm:tm + POOL_HALO, :]

    pending = {}

    def run_head(blk):
        pending[blk] = head(blk)

    def run_tail(blk):
        tail(blk, *pending.pop(blk))

    blocks = range(tm // Q_BLK)
    return ([functools.partial(run_head, blk) for blk in blocks],
            [functools.partial(run_tail, blk) for blk in blocks], finish)


def _mixer_weight_specs(l):
    return [pl.BlockSpec(memory_space=pltpu.SMEM),
            _layer_spec(l, (CONV_K, CONV_DIM)), _layer_spec(l, (1, CONV_DIM)),
            _layer_spec(l, (1, CONV_DIM)), _layer_spec(l, (1, CONV_DIM)),
            _layer_spec(l, (CONV_DIM, CONV_DIM)), _layer_spec(l, (POOL_DIM, POOL_DIM)),
            _layer_spec(l, (1, POOL_DIM))]


PROMPT_TILES = BATCH * SEQ // TM_MIX
TILES_PER_SEQ = SEQ // TM_MIX


def _prompt_mix_out_ffn_kernel(sinks_ref, z_ref, h_ref, dww_ref, dwb_ref, lng_ref, lnb_ref, pw_ref, poolw_ref,
                               pools_ref, ng_ref, wout_ref, wg_ref, wu_ref, wd_ref,
                               y_ref, nk_ref, nv_ref, nc_ref, np_ref,
                               kx_ref, vx_ref, gx_ref, px_ref, mix_ref, h2_ref, hn_ref, *, layer):
    s = pl.program_id(0)
    i = lax.rem(jnp.minimum(s, PROMPT_TILES - 1), TILES_PER_SEQ)

    @pl.when(i == 0)
    def _():
        kx_ref[0:WINDOW, :] = jnp.zeros((WINDOW, KV_DIM), BF16)
        vx_ref[0:WINDOW, :] = jnp.zeros((WINDOW, KV_DIM), BF16)
        gx_ref[0:CONV_HALO, :] = jnp.zeros((CONV_HALO, CONV_DIM), F32)
        px_ref[0:POOL_HALO, :] = jnp.zeros((POOL_HALO, POOL_DIM), F32)

    @pl.when(s == 0)
    def _():
        h2_ref[...] = jnp.zeros(h2_ref.shape, h2_ref.dtype)
        hn_ref[...] = jnp.zeros(hn_ref.shape, hn_ref.dtype)

    heads, tails, finish = _prompt_mixer_pieces(i, sinks_ref, z_ref, dww_ref, dwb_ref, lng_ref, lnb_ref, pw_ref,
                                                poolw_ref, pools_ref, mix_ref, nk_ref, nv_ref, nc_ref, np_ref,
                                                kx_ref, vx_ref, gx_ref, px_ref, layer)
    ffn = _swiglu(hn_ref[...], wg_ref, wu_ref, wd_ref, FF_SPLITS_FUSED, [None] + heads, tails)
    finish()
    y_ref[...] = h2_ref[...] + 0.5 * _rms(ffn, ng_ref[5:6, :])
    m = jnp.dot(mix_ref[...], wout_ref[...], preferred_element_type=F32)
    h2 = h_ref[...] + _rms(m, ng_ref[3:4, :])
    h2_ref[...] = h2
    hn_ref[...] = _rms(h2, ng_ref[4:5, :]).astype(BF16)


def _prompt_mix_out_ffn(l, z, h, sinks, dww, dwb, lng, lnb, pw, poolw, pools, ng, wout, wg, wu, wd):
    last = PROMPT_TILES - 1
    cur = lambda s: (jnp.minimum(s, last), 0)
    prev = lambda s: (jnp.maximum(s - 1, 0), 0)
    seq3 = lambda s: (jnp.minimum(s, last) // TILES_PER_SEQ, 0, 0)
    wspecs = _mixer_weight_specs(l)
    return pl.pallas_call(
        functools.partial(_prompt_mix_out_ffn_kernel, layer=l),
        grid=(PROMPT_TILES + 1,),
        in_specs=[wspecs[0], pl.BlockSpec((TM_MIX, IN_DIM), cur), pl.BlockSpec((TM_MIX, D_MODEL), cur)]
                 + wspecs[1:]
                 + [_layer_spec(l, (6, D_MODEL), True), _layer_spec(l, (MIX_DIM, D_MODEL), True),
                    _layer_spec(l, (D_MODEL, D_FF), True), _layer_spec(l, (D_MODEL, D_FF), True),
                    _layer_spec(l, (D_FF, D_MODEL), True)],
        out_specs=[pl.BlockSpec((TM_MIX, D_MODEL), prev),
                   pl.BlockSpec((None, WINDOW, KV_DIM), seq3), pl.BlockSpec((None, WINDOW, KV_DIM), seq3),
                   pl.BlockSpec((None, CONV_K - 1, CONV_DIM), seq3),
                   pl.BlockSpec((None, POOL_MAX - 1, POOL_DIM), seq3)],
        out_shape=[jax.ShapeDtypeStruct((BATCH * SEQ, D_MODEL), F32),
                   jax.ShapeDtypeStruct((BATCH, WINDOW, KV_DIM), F32),
                   jax.ShapeDtypeStruct((BATCH, WINDOW, KV_DIM), F32),
                   jax.ShapeDtypeStruct((BATCH, CONV_K - 1, CONV_DIM), F32),
                   jax.ShapeDtypeStruct((BATCH, POOL_MAX - 1, POOL_DIM), F32)],
        scratch_shapes=[pltpu.VMEM((WINDOW + TM_MIX, KV_DIM), BF16),
                        pltpu.VMEM((WINDOW + TM_MIX, KV_DIM), BF16),
                        pltpu.VMEM((CONV_HALO + TM_MIX, CONV_DIM), F32),
                        pltpu.VMEM((POOL_HALO + TM_MIX, POOL_DIM), F32),
                        pltpu.VMEM((TM_MIX, MIX_DIM), BF16),
                        pltpu.VMEM((TM_MIX, D_MODEL), F32),
                        pltpu.VMEM((TM_MIX, D_MODEL), BF16)],
        compiler_params=pltpu.CompilerParams(dimension_semantics=("arbitrary",),
                                             vmem_limit_bytes=VMEM_LIMIT_FUSED),
        name="prompt_mix_out_ffn",
    )(sinks, z, h, dww, dwb, lng, lnb, pw, poolw, pools, ng, wout, wg, wu, wd)


def _sample_mixer_kernel(sinks_ref, z3_ref, z2_ref, sk_ref, sv_ref, sc_ref, sp_ref,
                         dww_ref, dwb_ref, lng_ref, lnb_ref, pw_ref, poolw_ref, pools_ref,
                         ck_ref, cv_ref, cc_ref, cp_ref,
                         mix_ref, nk_ref, nv_ref, nc_ref, np_ref,
                         kx_ref, vx_ref, gx_ref, px_ref, re_ref, *, layer):
    del ck_ref, cv_ref, cc_ref, cp_ref
    nb, t = SEQ_BLK, DEC_SEQ
    lane = lax.broadcasted_iota(jnp.int32, (KV_DIM, LANES), 1)

    for x_ref, s_ref, n_ref, off in ((kx_ref, sk_ref, nk_ref, K_OFF), (vx_ref, sv_ref, nv_ref, V_OFF)):
        new_t = z2_ref[:, off:off + KV_DIM].T
        for b in range(nb):
            old = jnp.concatenate([s_ref[b, kv] for kv in range(N_KV_HEADS)], axis=0)
            x_ref[b, :, 0:WINDOW] = old.astype(BF16)
            fresh = pltpu.roll(new_t, (LANES - b * t) % LANES, 1)
            x_ref[b, :, WINDOW:] = jnp.where(lane < t, fresh, 0.0).astype(BF16)
            shifted = jnp.where(lane < WINDOW - t, pltpu.roll(old, WINDOW - t, 1),
                                pltpu.roll(new_t, (WINDOW - t - b * t) % LANES, 1))
            for kv in range(N_KV_HEADS):
                n_ref[b, kv] = shifted[kv * HEAD_DIM:(kv + 1) * HEAD_DIM, :]

    q_io = lax.broadcasted_iota(jnp.int32, (t, S_KEYS), 0)
    s_io = lax.broadcasted_iota(jnp.int32, (t, S_KEYS), 1)
    dist = q_io + WINDOW - s_io
    valid = (dist >= 0) & (dist < WINDOW) & (s_io + (PAST_LEN - WINDOW) >= 0)
    bias = _alibi_bias(dist.astype(F32))
    sinks = [sinks_ref[layer, h] * LOG2E for h in range(N_HEADS)]

    qs = _stack_q_heads(z3_ref[:, :, Q_OFF:Q_OFF + ATTN_DIM])
    s = jnp.einsum('bqd,bdk->bqk', qs, kx_ref[...], preferred_element_type=F32)
    es, rinv = _softmax_heads(s, valid, bias, sinks, t)
    o = jnp.einsum('bqk,bdk->bqd', jnp.concatenate(es, axis=1), vx_ref[...], preferred_element_type=F32)
    o_heads = [o[:, h * t:(h + 1) * t, :] * rinv[h] for h in range(N_HEADS)]
    attn = _unstack_heads(o_heads)
    mix_ref[:, 0:ATTN_DIM] = attn.reshape(nb * t, ATTN_DIM)

    def to_token_major(x, dst_ref, first):
        for cb in range(x.shape[1] // LANES):
            re_ref[cb] = x[:, cb * LANES:(cb + 1) * LANES]
        for tok in range(t):
            for cb in range(x.shape[1] // LANES):
                dst_ref[first + tok, :, cb * LANES:(cb + 1) * LANES] = re_ref[cb, pl.ds(tok, nb, stride=t), :]

    def to_seq_major(x):
        for tok in range(t):
            for cb in range(x.shape[1] // LANES):
                re_ref[cb, pl.ds(tok, nb, stride=t), :] = x[tok * nb:(tok + 1) * nb, cb * LANES:(cb + 1) * LANES]
        return jnp.concatenate([re_ref[cb] for cb in range(x.shape[1] // LANES)], axis=1)

    hist = CONV_K - 1
    gx_ref[0:hist] = sc_ref[...]
    to_token_major(z2_ref[:, GA_OFF:GA_OFF + CONV_DIM] * jax.nn.sigmoid(z2_ref[:, GB_OFF:GB_OFF + CONV_DIM]),
                   gx_ref, hist)
    nc_ref[...] = gx_ref[t:t + hist]
    ys = []
    for tok in range(t):
        c = jnp.broadcast_to(dwb_ref[...], (nb, CONV_DIM))
        for j in range(CONV_K):
            c = c + dww_ref[j:j + 1, :] * gx_ref[tok + j]
        ys.append(_conv_ln_act(c, lng_ref, lnb_ref))
    conv = jnp.dot(jnp.concatenate(ys, axis=0), pw_ref[...], preferred_element_type=F32)
    mix_ref[:, CONV_OUT_OFF:CONV_OUT_OFF + CONV_DIM] = to_seq_major(conv)

    hist = POOL_MAX - 1
    px_ref[0:hist] = sp_ref[...]
    to_token_major(z2_ref[:, PU_OFF:PU_OFF + POOL_DIM], px_ref, hist)
    np_ref[...] = px_ref[t:t + hist]
    w_lane = _pool_window_lane()
    ds = []
    for tok in range(t):
        cur = px_ref[hist + tok]
        run = cur
        sums = []
        for back in range(1, POOL_MAX):
            run = run + px_ref[hist + tok - back]
            if back + 1 in POOL_WINDOWS:
                sums.append(run)
        cnt = jnp.minimum(w_lane, PAST_LEN + tok + 1).astype(F32)
        ds.append(_pool_diff(sums, cnt, cur))
    pool = jnp.dot(jnp.concatenate(ds, axis=0), poolw_ref[...], preferred_element_type=F32) * pools_ref[...]
    mix_ref[:, POOL_OUT_OFF:POOL_OUT_OFF + POOL_DIM] = to_seq_major(pool)


def _sample_mix_out_ffn_kernel(sinks_ref, z3_ref, z2_ref, sk_ref, sv_ref, sc_ref, sp_ref,
                               dww_ref, dwb_ref, lng_ref, lnb_ref, pw_ref, poolw_ref, pools_ref,
                               ck_ref, cv_ref, cc_ref, cp_ref, h_ref, ng_ref, wout_ref, wg_ref, wu_ref, wd_ref,
                               y_ref, nk_ref, nv_ref, nc_ref, np_ref,
                               kx_ref, vx_ref, gx_ref, px_ref, re_ref, mix_ref, *, layer):
    _sample_mixer_kernel(sinks_ref, z3_ref, z2_ref, sk_ref, sv_ref, sc_ref, sp_ref,
                         dww_ref, dwb_ref, lng_ref, lnb_ref, pw_ref, poolw_ref, pools_ref,
                         ck_ref, cv_ref, cc_ref, cp_ref, mix_ref, nk_ref, nv_ref, nc_ref, np_ref,
                         kx_ref, vx_ref, gx_ref, px_ref, re_ref, layer=layer)
    _out_ffn_kernel(h_ref, mix_ref, ng_ref, wout_ref, wg_ref, wu_ref, wd_ref, y_ref, FF_SPLITS)


def _sample_mix_out_ffn(l, z, h, sk, sv, sc, sp, new_states, sinks, dww, dwb, lng, lnb, pw, poolw, pools,
                        ng, wout, wg, wu, wd):
    nb = SEQ_BLK
    rows = nb * DEC_SEQ
    assert rows == LANES and S_KEYS == 2 * LANES and WINDOW == LANES
    kv_spec = pl.BlockSpec((None, nb, N_KV_HEADS, HEAD_DIM, WINDOW), lambda j: (l, j, 0, 0, 0))
    rows_spec = lambda r, ch: pl.BlockSpec((None, r, nb, ch), lambda j: (l, 0, j, 0))
    wspecs = _mixer_weight_specs(l)
    first = BATCH * SEQ // rows
    operands = (sinks, z.reshape(-1, DEC_SEQ, IN_DIM), z, sk, sv, sc, sp, dww, dwb, lng, lnb, pw, poolw, pools)
    carried = tuple(new_states)
    return pl.pallas_call(
        functools.partial(_sample_mix_out_ffn_kernel, layer=l),
        grid=(DEC_BATCH // nb,),
        in_specs=[wspecs[0],
                  pl.BlockSpec((nb, DEC_SEQ, IN_DIM), lambda j: (first + j, 0, 0)),
                  pl.BlockSpec((rows, IN_DIM), lambda j: (first + j, 0)),
                  kv_spec, kv_spec, rows_spec(CONV_K - 1, CONV_DIM), rows_spec(POOL_MAX - 1, POOL_DIM)]
                 + wspecs[1:] + [pl.BlockSpec(memory_space=pl.ANY)] * len(carried)
                 + [pl.BlockSpec((rows, D_MODEL), lambda j: (first + j, 0)),
                    _layer_spec(l, (6, D_MODEL), True), _layer_spec(l, (MIX_DIM, D_MODEL), True),
                    _layer_spec(l, (D_MODEL, D_FF), True), _layer_spec(l, (D_MODEL, D_FF), True),
                    _layer_spec(l, (D_FF, D_MODEL), True)],
        out_specs=[pl.BlockSpec((rows, D_MODEL), lambda j: (j, 0)),
                   kv_spec, kv_spec, rows_spec(CONV_K - 1, CONV_DIM), rows_spec(POOL_MAX - 1, POOL_DIM)],
        out_shape=[jax.ShapeDtypeStruct((DEC_BATCH * DEC_SEQ, D_MODEL), F32),
                   jax.ShapeDtypeStruct(sk.shape, F32), jax.ShapeDtypeStruct(sv.shape, F32),
                   jax.ShapeDtypeStruct(sc.shape, F32), jax.ShapeDtypeStruct(sp.shape, F32)],
        input_output_aliases={len(operands) + n: 1 + n for n in range(len(carried))},
        scratch_shapes=[pltpu.VMEM((nb, KV_DIM, S_KEYS), BF16),
                        pltpu.VMEM((nb, KV_DIM, S_KEYS), BF16),
                        pltpu.VMEM((CONV_K - 1 + DEC_SEQ, nb, CONV_DIM), F32),
                        pltpu.VMEM((POOL_MAX - 1 + DEC_SEQ, nb, POOL_DIM), F32),
                        pltpu.VMEM((max(CONV_DIM, POOL_DIM) // LANES, rows, LANES), F32),
                        pltpu.VMEM((rows, MIX_DIM), F32)],
        compiler_params=pltpu.CompilerParams(dimension_semantics=("arbitrary",),
                                             vmem_limit_bytes=VMEM_LIMIT),
        name="sample_mix_out_ffn",
    )(*operands, *carried, h, ng, wout, wg, wu, wd)


def _block_diag(pool_w):
    eye = jnp.eye(N_POOL_GROUPS, dtype=pool_w.dtype)
    return jnp.einsum('lgcd,gh->lgchd', pool_w, eye).reshape(DEPTH, POOL_DIM, POOL_DIM)


def kernel(x_prompt, x_sample, state_attn_k, state_attn_v, state_conv, state_pool, norm_g, ffn1_wg, ffn1_wu, ffn1_wd, w_in, attn_sinks, conv_dw_w, conv_dw_b, conv_ln_g, conv_ln_b, conv_pw_w, pool_w, pool_scale, w_out, ffn2_wg, ffn2_wu, ffn2_wd):
    hp = x_prompt.reshape(BATCH * SEQ, D_MODEL)
    hs = x_sample.reshape(DEC_BATCH * DEC_SEQ, D_MODEL)
    sk_all = state_attn_k.transpose(0, 1, 3, 4, 2)
    sv_all = state_attn_v.transpose(0, 1, 3, 4, 2)
    sc_all = state_conv.transpose(0, 2, 1, 3)
    sp_all = state_pool.transpose(0, 2, 1, 3)
    f1 = (ffn1_wg, ffn1_wu, ffn1_wd)
    f2 = (ffn2_wg.astype(BF16), ffn2_wu.astype(BF16), ffn2_wd.astype(BF16))
    win = w_in
    wout = w_out.astype(BF16)
    mixer_w = (attn_sinks, conv_dw_w, conv_dw_b.reshape(DEPTH, 1, CONV_DIM),
               conv_ln_g.reshape(DEPTH, 1, CONV_DIM), conv_ln_b.reshape(DEPTH, 1, CONV_DIM),
               conv_pw_w.astype(BF16), _block_diag(pool_w).astype(BF16),
               pool_scale.reshape(DEPTH, 1, POOL_DIM))
    outs = [[] for _ in range(4)]
    sample_states = [jnp.zeros_like(a) for a in (sk_all, sv_all, sc_all, sp_all)]
    for l in range(DEPTH):
        h1, z = _ffn_in(l, hp, hs, norm_g, *f1, win)
        hp, k1, v1, c1, p1 = _prompt_mix_out_ffn(l, z, h1, *mixer_w, norm_g, wout, *f2)
        hs, *sample_states = _sample_mix_out_ffn(l, z, h1, sk_all, sv_all, sc_all, sp_all, sample_states,
                                                 *mixer_w, norm_g, wout, *f2)
        for lst, val in zip(outs, (k1, v1, c1, p1)):
            lst.append(val)

    kv_p = (DEPTH, BATCH, WINDOW, N_KV_HEADS, HEAD_DIM)
    k2, v2, c2, p2 = sample_states
    return (hp.reshape(BATCH, SEQ, D_MODEL), hs.reshape(DEC_BATCH, DEC_SEQ, D_MODEL),
            jnp.stack(outs[0]).reshape(kv_p), jnp.stack(outs[1]).reshape(kv_p),
            jnp.stack(outs[2]), jnp.stack(outs[3]),
            k2.transpose(0, 1, 4, 2, 3), v2.transpose(0, 1, 4, 2, 3),
            c2.transpose(0, 2, 1, 3), p2.transpose(0, 2, 1, 3))
```

```python
import functools

import numpy as np
import jax
import jax.numpy as jnp
from jax import lax
from jax.experimental import pallas as pl
from jax.experimental.pallas import tpu as pltpu

D_MODEL = 1024
BATCH = 8
SEQ = 2048
DEPTH = 4
DEC_BATCH = 128
DEC_SEQ = 8
PAST_LEN = 8192

N_HEADS = 8
HEAD_DIM = 64
N_KV_HEADS = 2
GROUP = N_HEADS // N_KV_HEADS
ATTN_DIM = N_HEADS * HEAD_DIM
KV_DIM = N_KV_HEADS * HEAD_DIM
WINDOW = 128
CONV_DIM = D_MODEL // 4
CONV_K = 31
POOL_DIM = D_MODEL // 4
POOL_WINDOWS = (2, 4, 8, 16)
N_POOL_GROUPS = 4
POOL_GROUP_DIM = POOL_DIM // N_POOL_GROUPS
POOL_MAX = 16
IN_DIM = ATTN_DIM + 2 * KV_DIM + 2 * CONV_DIM + POOL_DIM
MIX_DIM = ATTN_DIM + CONV_DIM + POOL_DIM
D_FF = ((8 * D_MODEL // 3 + 127) // 128) * 128
RMS_EPS = 1e-6
LN_EPS = 1e-5
NEG_INF = -1e30

Q_OFF = 0
K_OFF = ATTN_DIM
V_OFF = K_OFF + KV_DIM
GA_OFF = V_OFF + KV_DIM
GB_OFF = GA_OFF + CONV_DIM
PU_OFF = GB_OFF + CONV_DIM
CONV_OUT_OFF = ATTN_DIM
POOL_OUT_OFF = ATTN_DIM + CONV_DIM

LANES = 128
SUBLANES = 8
TM_FFN = 512
FFN_SUBTILES = 2
CAST_ROWS = 128
CAST_SLOTS = 4
SIDE_BLOCKS = 32
SIDE_BLOCKS_DOWN = 16
MXU_TILE = 256
FF_SPLITS = (4 * MXU_TILE, 4 * MXU_TILE, 3 * MXU_TILE)
FF_SPLITS_FUSED = (1 * MXU_TILE, 3 * MXU_TILE, 2 * MXU_TILE, 2 * MXU_TILE, 3 * MXU_TILE)
TM_MIX = 512
Q_BLK = 128
CONV_ROWS = 64
CONV_HALO = 32
POOL_HALO = 16
SEQ_BLK = 16
S_KEYS = 256
VMEM_LIMIT = 48 * 1024 * 1024
VMEM_LIMIT_FUSED = 54 * 1024 * 1024

LOG2E = float(np.log2(np.e))
ALIBI_SLOPES2 = [float(s) * LOG2E
                 for s in np.exp2(-8.0 * np.arange(1, N_HEADS + 1, dtype=np.float32) / N_HEADS)]
Q_SCALE2 = HEAD_DIM ** -0.5 * LOG2E

F32 = jnp.float32
BF16 = jnp.bfloat16


def _rms(x, g):
    return x * lax.rsqrt(jnp.mean(x * x, axis=-1, keepdims=True) + RMS_EPS) * g


def _swiglu(xn, wg_ref, wu_ref, wd_ref, splits=(D_FF,), before=(), after=()):
    bounds = np.cumsum((0,) + tuple(splits))
    acc = None
    act = None
    for c in range(len(splits) + 1):
        if c < len(splits):
            if c < len(before) and before[c] is not None:
                before[c]()
            sl = slice(int(bounds[c]), int(bounds[c + 1]))
            g = jnp.dot(xn, wg_ref[:, sl], preferred_element_type=F32)
            u = jnp.dot(xn, wu_ref[:, sl], preferred_element_type=F32)
            nxt = (g * jax.nn.sigmoid(g) * u).astype(BF16)
        if c > 0:
            sl = slice(int(bounds[c - 1]), int(bounds[c]))
            part = jnp.dot(act, wd_ref[sl, :], preferred_element_type=F32)
            acc = part if acc is None else acc + part
            if c - 1 < len(after) and after[c - 1] is not None:
                after[c - 1]()
        act = nxt
    return acc


def _load_weights_as_bf16(layer, pairs, stage_ref, sem_ref):
    slots = stage_ref.shape[0]
    chunks = [(src, dst, r0) for src, dst in pairs for r0 in range(0, dst.shape[0], CAST_ROWS)]

    def chunk_copy(idx):
        src, dst, r0 = chunks[idx]
        return pltpu.make_async_copy(src.at[layer, pl.ds(r0, CAST_ROWS), :],
                                     stage_ref.at[idx % slots, :, 0:dst.shape[1]], sem_ref.at[idx % slots])

    for idx in range(slots - 1):
        chunk_copy(idx).start()
    for idx, (src, dst, r0) in enumerate(chunks):
        if idx + slots - 1 < len(chunks):
            chunk_copy(idx + slots - 1).start()
        chunk_copy(idx).wait()
        dst[r0:r0 + CAST_ROWS, :] = stage_ref[idx % slots, :, 0:dst.shape[1]].astype(BF16)


def _ffn_in_kernel(xp_ref, xs_ref, ng_ref, wg_hbm, wu_hbm, wd_hbm, win_hbm,
                   wout2_ref, wg2_ref, wu2_ref, wd2_ref, h_ref, z_ref, wout2b_ref, wg2b_ref, wu2b_ref, wd2b_ref,
                   wg_ref, wu_ref, wd_ref, win_ref, stage_ref, sem_ref, *, layer):
    @pl.when(pl.program_id(0) == 0)
    def _():
        _load_weights_as_bf16(layer, ((wg_hbm, wg_ref), (wu_hbm, wu_ref), (wd_hbm, wd_ref), (win_hbm, win_ref)),
                              stage_ref, sem_ref)

    for src, dst in ((wout2_ref, wout2b_ref), (wg2_ref, wg2b_ref), (wu2_ref, wu2b_ref), (wd2_ref, wd2b_ref)):
        dst[...] = src[...].astype(BF16)

    from_prompt = pl.program_id(0) < PROMPT_FFN_TILES
    sub = xp_ref.shape[0] // FFN_SUBTILES
    rows = [slice(k * sub, (k + 1) * sub) for k in range(FFN_SUBTILES)]
    xs = [jnp.where(from_prompt, xp_ref[r, :], xs_ref[r, :]) for r in rows]
    xns = [_rms(x, ng_ref[0:1, :]).astype(BF16) for x in xs]
    us = []
    for k in range(FFN_SUBTILES):
        ffn = _swiglu(xns[k], wg_ref, wu_ref, wd_ref, FF_SPLITS)
        h = xs[k] + 0.5 * _rms(ffn, ng_ref[1:2, :])
        h_ref[rows[k], :] = h
        us.append(_rms(h, ng_ref[2:3, :]).astype(BF16))
        if k > 0:
            z_ref[rows[k - 1], :] = jnp.dot(us[k - 1], win_ref[...], preferred_element_type=F32)
    z_ref[rows[-1], :] = jnp.dot(us[-1], win_ref[...], preferred_element_type=F32)


def _out_ffn_kernel(h_ref, mix_ref, ng_ref, wout_ref, wg_ref, wu_ref, wd_ref, y_ref,
                    splits=(D_FF,), before=(), after=()):
    m = jnp.dot(mix_ref[...].astype(BF16), wout_ref[...], preferred_element_type=F32)
    h = h_ref[...] + _rms(m, ng_ref[3:4, :])
    hn = _rms(h, ng_ref[4:5, :]).astype(BF16)
    y_ref[...] = h + 0.5 * _rms(_swiglu(hn, wg_ref, wu_ref, wd_ref, splits, before, after), ng_ref[5:6, :])


def _const_spec(shape):
    return pl.BlockSpec(shape, lambda *_: (0,) * len(shape), pipeline_mode=pl.Buffered(1))


def _layer_spec(l, shape, single_buffer=False):
    index_map = lambda *_: (l,) + (0,) * len(shape)
    if single_buffer:
        return pl.BlockSpec((None,) + shape, index_map, pipeline_mode=pl.Buffered(1))
    return pl.BlockSpec((None,) + shape, index_map)


PROMPT_FFN_TILES = BATCH * SEQ // TM_FFN
SAMPLE_FFN_TILES = DEC_BATCH * DEC_SEQ // TM_FFN


def _ffn_in(l, xp, xs, ng, wg, wu, wd, win, wout2, wg2, wu2, wd2):
    n = xp.shape[0] + xs.shape[0]
    row = lambda i: (i, 0)
    side_in, side_out, side_shape = [], [], []
    for w, blocks in ((wout2, SIDE_BLOCKS), (wg2, SIDE_BLOCKS), (wu2, SIDE_BLOCKS), (wd2, SIDE_BLOCKS_DOWN)):
        k_dim, n_dim = w.shape[1:]
        blk = (k_dim // blocks, n_dim)
        side_in.append(pl.BlockSpec((None,) + blk, lambda i, b=blocks: (l, jnp.minimum(i, b - 1), 0)))
        side_out.append(pl.BlockSpec(blk, lambda i, b=blocks: (jnp.minimum(i, b - 1), 0)))
        side_shape.append(jax.ShapeDtypeStruct((k_dim, n_dim), BF16))
    return pl.pallas_call(
        functools.partial(_ffn_in_kernel, layer=l),
        grid=(PROMPT_FFN_TILES + SAMPLE_FFN_TILES,),
        in_specs=[pl.BlockSpec((TM_FFN, D_MODEL), lambda i: (jnp.minimum(i, PROMPT_FFN_TILES - 1), 0)),
                  pl.BlockSpec((TM_FFN, D_MODEL), lambda i: (jnp.maximum(i - PROMPT_FFN_TILES, 0), 0)),
                  _layer_spec(l, (6, D_MODEL), True)] + [pl.BlockSpec(memory_space=pl.ANY)] * 4 + side_in,
        out_specs=[pl.BlockSpec((TM_FFN, D_MODEL), row), pl.BlockSpec((TM_FFN, IN_DIM), row)] + side_out,
        out_shape=[jax.ShapeDtypeStruct((n, D_MODEL), F32), jax.ShapeDtypeStruct((n, IN_DIM), F32)] + side_shape,
        scratch_shapes=[pltpu.VMEM((D_MODEL, D_FF), BF16), pltpu.VMEM((D_MODEL, D_FF), BF16),
                        pltpu.VMEM((D_FF, D_MODEL), BF16), pltpu.VMEM((D_MODEL, IN_DIM), BF16),
                        pltpu.VMEM((CAST_SLOTS, CAST_ROWS, D_FF), F32), pltpu.SemaphoreType.DMA((CAST_SLOTS,))],
        compiler_params=pltpu.CompilerParams(dimension_semantics=("arbitrary",),
                                             vmem_limit_bytes=VMEM_LIMIT_FUSED),
        name="ffn_in",
    )(xp, xs, ng, wg, wu, wd, win, wout2, wg2, wu2, wd2)


def _stack_q_heads(q):
    lane = lax.broadcasted_iota(jnp.int32, q.shape[:-1] + (LANES,), q.ndim - 1)
    pieces = []
    for h in range(N_HEADS):
        kvh = h // GROUP
        slab = q[..., (h // 2) * LANES:(h // 2 + 1) * LANES]
        if h % 2 != kvh:
            slab = pltpu.roll(slab, HEAD_DIM, q.ndim - 1)
        keep = (lane >= kvh * HEAD_DIM) & (lane < (kvh + 1) * HEAD_DIM)
        pieces.append((jnp.where(keep, slab, 0.0) * Q_SCALE2).astype(BF16))
    return jnp.concatenate(pieces, axis=-2)


def _unstack_heads(o_heads):
    ndim = o_heads[0].ndim
    lane = lax.broadcasted_iota(jnp.int32, o_heads[0].shape, ndim - 1)
    slabs = []
    for j in range(N_HEADS // 2):
        a, b = o_heads[2 * j], o_heads[2 * j + 1]
        kvh = (2 * j) // GROUP
        if kvh == 0:
            b = pltpu.roll(b, HEAD_DIM, ndim - 1)
        else:
            a = pltpu.roll(a, HEAD_DIM, ndim - 1)
        slabs.append(jnp.where(lane < HEAD_DIM, a, b))
    return jnp.concatenate(slabs, axis=-1)


def _alibi_bias(dist_f):
    return [ALIBI_SLOPES2[h] * dist_f for h in range(N_HEADS)]


def _softmax_heads(s, valid, bias, sinks, rows):
    es, rinv = [], []
    for h in range(N_HEADS):
        sh = s[..., h * rows:(h + 1) * rows, :]
        l = jnp.where(valid, sh - bias[h], NEG_INF)
        m = jnp.maximum(jnp.max(l, axis=-1, keepdims=True), sinks[h])
        e = jnp.exp2(l - m)
        den = jnp.sum(e, axis=-1, keepdims=True) + jnp.exp2(sinks[h] - m)
        es.append(e.astype(BF16))
        rinv.append(1.0 / den)
    return es, rinv


def _conv_ln_act(c, lng_ref, lnb_ref):
    mu = jnp.mean(c, axis=-1, keepdims=True)
    xc = c - mu
    y = xc * lax.rsqrt(jnp.mean(xc * xc, axis=-1, keepdims=True) + LN_EPS) * lng_ref[...] + lnb_ref[...]
    return (y * jax.nn.sigmoid(y)).astype(BF16)


def _conv_tail(c, lng_ref, lnb_ref, pw_ref):
    return jnp.dot(_conv_ln_act(c, lng_ref, lnb_ref), pw_ref[...], preferred_element_type=F32)


def _pool_diff(sums, cnt, cur):
    lane = lax.broadcasted_iota(jnp.int32, cur.shape, cur.ndim - 1)
    wsum = sums[-1]
    for g in range(N_POOL_GROUPS - 2, -1, -1):
        wsum = jnp.where(lane < (g + 1) * POOL_GROUP_DIM, sums[g], wsum)
    return (wsum / cnt - cur).astype(BF16)


def _pool_tail(sums, cnt, cur, poolw_ref, pools_ref):
    return jnp.dot(_pool_diff(sums, cnt, cur), poolw_ref[...], preferred_element_type=F32) * pools_ref[...]


def _pool_window_lane():
    lane = lax.broadcasted_iota(jnp.int32, (1, POOL_DIM), 1)
    w = jnp.full((1, POOL_DIM), POOL_WINDOWS[-1], jnp.int32)
    for g in range(N_POOL_GROUPS - 2, -1, -1):
        w = jnp.where(lane < (g + 1) * POOL_GROUP_DIM, POOL_WINDOWS[g], w)
    return w


def _dw_conv_rows(gx_ref, dww_ref, r0, rows):
    first = CONV_HALO - (CONV_K - 1)
    outs = []
    for c0 in range(0, CONV_DIM, LANES):
        cols = slice(c0, c0 + LANES)
        out = None
        for r in range(SUBLANES):
            ext = rows if r == 0 else rows + SUBLANES
            acc = None
            for m in range((first + CONV_K - 1) // SUBLANES + 1):
                j = SUBLANES * m + r - first
                if j < 0 or j >= CONV_K:
                    continue
                lo = r0 + SUBLANES * m
                term = dww_ref[j:j + 1, cols] * gx_ref[lo:lo + ext, cols]
                acc = term if acc is None else acc + term
            part = acc if r == 0 else acc[r:r + rows, :]
            out = part if out is None else out + part
        outs.append(out)
    return jnp.concatenate(outs, axis=1)


def _prompt_mixer_pieces(i, sinks_ref, z_ref, dww_ref, dwb_ref, lng_ref, lnb_ref, pw_ref, poolw_ref, pools_ref,
                         mix_ref, nk_ref, nv_ref, nc_ref, np_ref,
                         kx_ref, vx_ref, gx_ref, px_ref, layer):
    tm = TM_MIX

    kx_ref[WINDOW:, :] = z_ref[:, K_OFF:K_OFF + KV_DIM].astype(BF16)
    vx_ref[WINDOW:, :] = z_ref[:, V_OFF:V_OFF + KV_DIM].astype(BF16)
    gx_ref[CONV_HALO:, :] = z_ref[:, GA_OFF:GA_OFF + CONV_DIM] * jax.nn.sigmoid(z_ref[:, GB_OFF:GB_OFF + CONV_DIM])
    px_ref[POOL_HALO:, :] = z_ref[:, PU_OFF:PU_OFF + POOL_DIM]

    q_io = lax.broadcasted_iota(jnp.int32, (Q_BLK, WINDOW + Q_BLK), 0)
    s_io = lax.broadcasted_iota(jnp.int32, (Q_BLK, WINDOW + Q_BLK), 1)
    dist = q_io + WINDOW - s_io
    band = (dist >= 0) & (dist < WINDOW)
    bias = _alibi_bias(dist.astype(F32))
    sinks = [sinks_ref[layer, h] * LOG2E for h in range(N_HEADS)]
    w_lane = _pool_window_lane()

    def head(blk):
        r0 = blk * Q_BLK
        first_key_pos = i * tm + r0 - WINDOW
        valid = band & (s_io + first_key_pos >= 0)
        qs = _stack_q_heads(z_ref[r0:r0 + Q_BLK, Q_OFF:Q_OFF + ATTN_DIM])
        kb = kx_ref[r0:r0 + WINDOW + Q_BLK, :]
        s = lax.dot_general(qs, kb, (((1,), (1,)), ((), ())), preferred_element_type=F32)
        es, rinv = _softmax_heads(s, valid, bias, sinks, Q_BLK)
        ys = []
        for rc in range(Q_BLK // CONV_ROWS):
            c = _dw_conv_rows(gx_ref, dww_ref, r0 + rc * CONV_ROWS, CONV_ROWS) + dwb_ref[...]
            ys.append(_conv_ln_act(c, lng_ref, lnb_ref))
        run = px_ref[r0:r0 + POOL_HALO + Q_BLK, :]
        cur = run[POOL_HALO:, :]
        sums = []
        span = 1
        while span < POOL_MAX:
            run = run + pltpu.roll(run, span, 0)
            span *= 2
            if span in POOL_WINDOWS:
                sums.append(run[POOL_HALO:, :])
        pos = i * tm + r0 + lax.broadcasted_iota(jnp.int32, (Q_BLK, 1), 0)
        cnt = jnp.minimum(w_lane, pos + 1).astype(F32)
        return (jnp.concatenate(es, axis=0), rinv, jnp.concatenate(ys, axis=0), _pool_diff(sums, cnt, cur))

    def tail(blk, p, rinv, y, d):
        r0 = blk * Q_BLK
        o = jnp.dot(p, vx_ref[r0:r0 + WINDOW + Q_BLK, :], preferred_element_type=F32)
        o_heads = [o[h * Q_BLK:(h + 1) * Q_BLK, :] * rinv[h] for h in range(N_HEADS)]
        mix_ref[r0:r0 + Q_BLK, 0:ATTN_DIM] = _unstack_heads(o_heads).astype(mix_ref.dtype)
        conv = jnp.dot(y, pw_ref[...], preferred_element_type=F32)
        mix_ref[r0:r0 + Q_BLK, CONV_OUT_OFF:CONV_OUT_OFF + CONV_DIM] = conv.astype(mix_ref.dtype)
        pool = jnp.dot(d, poolw_ref[...], preferred_element_type=F32) * pools_ref[...]
        mix_ref[r0:r0 + Q_BLK, POOL_OUT_OFF:POOL_OUT_OFF + POOL_DIM] = pool.astype(mix_ref.dtype)

    def finish():
        nk_ref[...] = z_ref[tm - WINDOW:, K_OFF:K_OFF + KV_DIM]
        nv_ref[...] = z_ref[tm - WINDOW:, V_OFF:V_OFF + KV_DIM]
        nc_ref[...] = gx_ref[CONV_HALO + tm - (CONV_K - 1):, :]
        np_ref[...] = px_ref[POOL_HALO + tm - (POOL_MAX - 1):, :]
        kx_ref[0:WINDOW, :] = kx_ref[tm:tm + WINDOW, :]
        vx_ref[0:WINDOW, :] = vx_ref[tm:tm + WINDOW, :]
        gx_ref[0:CONV_HALO, :] = gx_ref[tm:tm + CONV_HALO, :]
        px_ref[0:POOL_HALO, :] = px_ref[tm:tm + POOL_HALO, :]

    pending = {}

    def run_head(blk):
        pending[blk] = head(blk)

    def run_tail(blk):
        tail(blk, *pending.pop(blk))

    blocks = range(tm // Q_BLK)
    return ([functools.partial(run_head, blk) for blk in blocks],
            [functools.partial(run_tail, blk) for blk in blocks], finish)


def _mixer_weight_specs(l):
    return [pl.BlockSpec(memory_space=pltpu.SMEM),
            _layer_spec(l, (CONV_K, CONV_DIM)), _layer_spec(l, (1, CONV_DIM)),
            _layer_spec(l, (1, CONV_DIM)), _layer_spec(l, (1, CONV_DIM)),
            _layer_spec(l, (CONV_DIM, CONV_DIM)), _layer_spec(l, (POOL_DIM, POOL_DIM)),
            _layer_spec(l, (1, POOL_DIM))]


PROMPT_TILES = BATCH * SEQ // TM_MIX
TILES_PER_SEQ = SEQ // TM_MIX


def _prompt_mix_out_ffn_kernel(sinks_ref, z_ref, h_ref, dww_ref, dwb_ref, lng_ref, lnb_ref, pw_ref, poolw_ref,
                               pools_ref, ng_ref, wout_ref, wg_ref, wu_ref, wd_ref,
                               y_ref, nk_ref, nv_ref, nc_ref, np_ref,
                               kx_ref, vx_ref, gx_ref, px_ref, mix_ref, h2_ref, hn_ref, *, layer):
    s = pl.program_id(0)
    i = lax.rem(jnp.minimum(s, PROMPT_TILES - 1), TILES_PER_SEQ)

    @pl.when(i == 0)
    def _():
        kx_ref[0:WINDOW, :] = jnp.zeros((WINDOW, KV_DIM), BF16)
        vx_ref[0:WINDOW, :] = jnp.zeros((WINDOW, KV_DIM), BF16)
        gx_ref[0:CONV_HALO, :] = jnp.zeros((CONV_HALO, CONV_DIM), F32)
        px_ref[0:POOL_HALO, :] = jnp.zeros((POOL_HALO, POOL_DIM), F32)

    @pl.when(s == 0)
    def _():
        h2_ref[...] = jnp.zeros(h2_ref.shape, h2_ref.dtype)
        hn_ref[...] = jnp.zeros(hn_ref.shape, hn_ref.dtype)

    heads, tails, finish = _prompt_mixer_pieces(i, sinks_ref, z_ref, dww_ref, dwb_ref, lng_ref, lnb_ref, pw_ref,
                                                poolw_ref, pools_ref, mix_ref, nk_ref, nv_ref, nc_ref, np_ref,
                                                kx_ref, vx_ref, gx_ref, px_ref, layer)
    ffn = _swiglu(hn_ref[...], wg_ref, wu_ref, wd_ref, FF_SPLITS_FUSED, [None] + heads, tails)
    finish()
    y_ref[...] = h2_ref[...] + 0.5 * _rms(ffn, ng_ref[5:6, :])
    m = jnp.dot(mix_ref[...], wout_ref[...], preferred_element_type=F32)
    h2 = h_ref[...] + _rms(m, ng_ref[3:4, :])
    h2_ref[...] = h2
    hn_ref[...] = _rms(h2, ng_ref[4:5, :]).astype(BF16)


def _prompt_mix_out_ffn(l, z, h, sinks, dww, dwb, lng, lnb, pw, poolw, pools, ng, wout, wg, wu, wd):
    last = PROMPT_TILES - 1
    cur = lambda s: (jnp.minimum(s, last), 0)
    prev = lambda s: (jnp.maximum(s - 1, 0), 0)
    seq3 = lambda s: (jnp.minimum(s, last) // TILES_PER_SEQ, 0, 0)
    wspecs = _mixer_weight_specs(l)
    return pl.pallas_call(
        functools.partial(_prompt_mix_out_ffn_kernel, layer=l),
        grid=(PROMPT_TILES + 1,),
        in_specs=[wspecs[0], pl.BlockSpec((TM_MIX, IN_DIM), cur), pl.BlockSpec((TM_MIX, D_MODEL), cur)]
                 + wspecs[1:]
                 + [_layer_spec(l, (6, D_MODEL), True), _const_spec((MIX_DIM, D_MODEL)),
                    _const_spec((D_MODEL, D_FF)), _const_spec((D_MODEL, D_FF)),
                    _const_spec((D_FF, D_MODEL))],
        out_specs=[pl.BlockSpec((TM_MIX, D_MODEL), prev),
                   pl.BlockSpec((None, WINDOW, KV_DIM), seq3), pl.BlockSpec((None, WINDOW, KV_DIM), seq3),
                   pl.BlockSpec((None, CONV_K - 1, CONV_DIM), seq3),
                   pl.BlockSpec((None, POOL_MAX - 1, POOL_DIM), seq3)],
        out_shape=[jax.ShapeDtypeStruct((BATCH * SEQ, D_MODEL), F32),
                   jax.ShapeDtypeStruct((BATCH, WINDOW, KV_DIM), F32),
                   jax.ShapeDtypeStruct((BATCH, WINDOW, KV_DIM), F32),
                   jax.ShapeDtypeStruct((BATCH, CONV_K - 1, CONV_DIM), F32),
                   jax.ShapeDtypeStruct((BATCH, POOL_MAX - 1, POOL_DIM), F32)],
        scratch_shapes=[pltpu.VMEM((WINDOW + TM_MIX, KV_DIM), BF16),
                        pltpu.VMEM((WINDOW + TM_MIX, KV_DIM), BF16),
                        pltpu.VMEM((CONV_HALO + TM_MIX, CONV_DIM), F32),
                        pltpu.VMEM((POOL_HALO + TM_MIX, POOL_DIM), F32),
                        pltpu.VMEM((TM_MIX, MIX_DIM), BF16),
                        pltpu.VMEM((TM_MIX, D_MODEL), F32),
                        pltpu.VMEM((TM_MIX, D_MODEL), BF16)],
        compiler_params=pltpu.CompilerParams(dimension_semantics=("arbitrary",),
                                             vmem_limit_bytes=VMEM_LIMIT_FUSED),
        name="prompt_mix_out_ffn",
    )(sinks, z, h, dww, dwb, lng, lnb, pw, poolw, pools, ng, wout, wg, wu, wd)


def _sample_mixer_kernel(sinks_ref, z3_ref, z2_ref, sk_ref, sv_ref, sc_ref, sp_ref,
                         dww_ref, dwb_ref, lng_ref, lnb_ref, pw_ref, poolw_ref, pools_ref,
                         ck_ref, cv_ref, cc_ref, cp_ref,
                         mix_ref, nk_ref, nv_ref, nc_ref, np_ref,
                         kx_ref, vx_ref, gx_ref, px_ref, re_ref, *, layer):
    del ck_ref, cv_ref, cc_ref, cp_ref
    nb, t = SEQ_BLK, DEC_SEQ
    lane = lax.broadcasted_iota(jnp.int32, (KV_DIM, LANES), 1)

    for x_ref, s_ref, n_ref, off in ((kx_ref, sk_ref, nk_ref, K_OFF), (vx_ref, sv_ref, nv_ref, V_OFF)):
        new_t = z2_ref[:, off:off + KV_DIM].T
        for b in range(nb):
            old = jnp.concatenate([s_ref[b, kv] for kv in range(N_KV_HEADS)], axis=0)
            x_ref[b, :, 0:WINDOW] = old.astype(BF16)
            fresh = pltpu.roll(new_t, (LANES - b * t) % LANES, 1)
            x_ref[b, :, WINDOW:] = jnp.where(lane < t, fresh, 0.0).astype(BF16)
            shifted = jnp.where(lane < WINDOW - t, pltpu.roll(old, WINDOW - t, 1),
                                pltpu.roll(new_t, (WINDOW - t - b * t) % LANES, 1))
            for kv in range(N_KV_HEADS):
                n_ref[b, kv] = shifted[kv * HEAD_DIM:(kv + 1) * HEAD_DIM, :]

    q_io = lax.broadcasted_iota(jnp.int32, (t, S_KEYS), 0)
    s_io = lax.broadcasted_iota(jnp.int32, (t, S_KEYS), 1)
    dist = q_io + WINDOW - s_io
    valid = (dist >= 0) & (dist < WINDOW) & (s_io + (PAST_LEN - WINDOW) >= 0)
    bias = _alibi_bias(dist.astype(F32))
    sinks = [sinks_ref[layer, h] * LOG2E for h in range(N_HEADS)]

    qs = _stack_q_heads(z3_ref[:, :, Q_OFF:Q_OFF + ATTN_DIM])
    s = jnp.einsum('bqd,bdk->bqk', qs, kx_ref[...], preferred_element_type=F32)
    es, rinv = _softmax_heads(s, valid, bias, sinks, t)
    o = jnp.einsum('bqk,bdk->bqd', jnp.concatenate(es, axis=1), vx_ref[...], preferred_element_type=F32)
    o_heads = [o[:, h * t:(h + 1) * t, :] * rinv[h] for h in range(N_HEADS)]
    attn = _unstack_heads(o_heads)
    mix_ref[:, 0:ATTN_DIM] = attn.reshape(nb * t, ATTN_DIM)

    def to_token_major(x, dst_ref, first):
        for cb in range(x.shape[1] // LANES):
            re_ref[cb] = x[:, cb * LANES:(cb + 1) * LANES]
        for tok in range(t):
            for cb in range(x.shape[1] // LANES):
                dst_ref[first + tok, :, cb * LANES:(cb + 1) * LANES] = re_ref[cb, pl.ds(tok, nb, stride=t), :]

    def to_seq_major(x):
        for tok in range(t):
            for cb in range(x.shape[1] // LANES):
                re_ref[cb, pl.ds(tok, nb, stride=t), :] = x[tok * nb:(tok + 1) * nb, cb * LANES:(cb + 1) * LANES]
        return jnp.concatenate([re_ref[cb] for cb in range(x.shape[1] // LANES)], axis=1)

    hist = CONV_K - 1
    gx_ref[0:hist] = sc_ref[...]
    to_token_major(z2_ref[:, GA_OFF:GA_OFF + CONV_DIM] * jax.nn.sigmoid(z2_ref[:, GB_OFF:GB_OFF + CONV_DIM]),
                   gx_ref, hist)
    nc_ref[...] = gx_ref[t:t + hist]
    ys = []
    for tok in range(t):
        c = jnp.broadcast_to(dwb_ref[...], (nb, CONV_DIM))
        for j in range(CONV_K):
            c = c + dww_ref[j:j + 1, :] * gx_ref[tok + j]
        ys.append(_conv_ln_act(c, lng_ref, lnb_ref))
    conv = jnp.dot(jnp.concatenate(ys, axis=0), pw_ref[...], preferred_element_type=F32)
    mix_ref[:, CONV_OUT_OFF:CONV_OUT_OFF + CONV_DIM] = to_seq_major(conv)

    hist = POOL_MAX - 1
    px_ref[0:hist] = sp_ref[...]
    to_token_major(z2_ref[:, PU_OFF:PU_OFF + POOL_DIM], px_ref, hist)
    np_ref[...] = px_ref[t:t + hist]
    w_lane = _pool_window_lane()
    ds = []
    for tok in range(t):
        cur = px_ref[hist + tok]
        run = cur
        sums = []
        for back in range(1, POOL_MAX):
            run = run + px_ref[hist + tok - back]
            if back + 1 in POOL_WINDOWS:
                sums.append(run)
        cnt = jnp.minimum(w_lane, PAST_LEN + tok + 1).astype(F32)
        ds.append(_pool_diff(sums, cnt, cur))
    pool = jnp.dot(jnp.concatenate(ds, axis=0), poolw_ref[...], preferred_element_type=F32) * pools_ref[...]
    mix_ref[:, POOL_OUT_OFF:POOL_OUT_OFF + POOL_DIM] = to_seq_major(pool)


def _sample_mix_out_ffn_kernel(sinks_ref, z3_ref, z2_ref, sk_ref, sv_ref, sc_ref, sp_ref,
                               dww_ref, dwb_ref, lng_ref, lnb_ref, pw_ref, poolw_ref, pools_ref,
                               ck_ref, cv_ref, cc_ref, cp_ref, h_ref, ng_ref, wout_ref, wg_ref, wu_ref, wd_ref,
                               y_ref, nk_ref, nv_ref, nc_ref, np_ref,
                               kx_ref, vx_ref, gx_ref, px_ref, re_ref, mix_ref, *, layer):
    _sample_mixer_kernel(sinks_ref, z3_ref, z2_ref, sk_ref, sv_ref, sc_ref, sp_ref,
                         dww_ref, dwb_ref, lng_ref, lnb_ref, pw_ref, poolw_ref, pools_ref,
                         ck_ref, cv_ref, cc_ref, cp_ref, mix_ref, nk_ref, nv_ref, nc_ref, np_ref,
                         kx_ref, vx_ref, gx_ref, px_ref, re_ref, layer=layer)
    _out_ffn_kernel(h_ref, mix_ref, ng_ref, wout_ref, wg_ref, wu_ref, wd_ref, y_ref, FF_SPLITS)


def _sample_mix_out_ffn(l, z, h, sk, sv, sc, sp, new_states, sinks, dww, dwb, lng, lnb, pw, poolw, pools,
                        ng, wout, wg, wu, wd):
    nb = SEQ_BLK
    rows = nb * DEC_SEQ
    assert rows == LANES and S_KEYS == 2 * LANES and WINDOW == LANES
    kv_spec = pl.BlockSpec((None, nb, N_KV_HEADS, HEAD_DIM, WINDOW), lambda j: (l, j, 0, 0, 0))
    rows_spec = lambda r, ch: pl.BlockSpec((None, r, nb, ch), lambda j: (l, 0, j, 0))
    wspecs = _mixer_weight_specs(l)
    first = BATCH * SEQ // rows
    operands = (sinks, z.reshape(-1, DEC_SEQ, IN_DIM), z, sk, sv, sc, sp, dww, dwb, lng, lnb, pw, poolw, pools)
    carried = tuple(new_states)
    return pl.pallas_call(
        functools.partial(_sample_mix_out_ffn_kernel, layer=l),
        grid=(DEC_BATCH // nb,),
        in_specs=[wspecs[0],
                  pl.BlockSpec((nb, DEC_SEQ, IN_DIM), lambda j: (first + j, 0, 0)),
                  pl.BlockSpec((rows, IN_DIM), lambda j: (first + j, 0)),
                  kv_spec, kv_spec, rows_spec(CONV_K - 1, CONV_DIM), rows_spec(POOL_MAX - 1, POOL_DIM)]
                 + wspecs[1:] + [pl.BlockSpec(memory_space=pl.ANY)] * len(carried)
                 + [pl.BlockSpec((rows, D_MODEL), lambda j: (first + j, 0)),
                    _layer_spec(l, (6, D_MODEL), True), _const_spec((MIX_DIM, D_MODEL)),
                    _const_spec((D_MODEL, D_FF)), _const_spec((D_MODEL, D_FF)),
                    _const_spec((D_FF, D_MODEL))],
        out_specs=[pl.BlockSpec((rows, D_MODEL), lambda j: (j, 0)),
                   kv_spec, kv_spec, rows_spec(CONV_K - 1, CONV_DIM), rows_spec(POOL_MAX - 1, POOL_DIM)],
        out_shape=[jax.ShapeDtypeStruct((DEC_BATCH * DEC_SEQ, D_MODEL), F32),
                   jax.ShapeDtypeStruct(sk.shape, F32), jax.ShapeDtypeStruct(sv.shape, F32),
                   jax.ShapeDtypeStruct(sc.shape, F32), jax.ShapeDtypeStruct(sp.shape, F32)],
        input_output_aliases={len(operands) + n: 1 + n for n in range(len(carried))},
        scratch_shapes=[pltpu.VMEM((nb, KV_DIM, S_KEYS), BF16),
                        pltpu.VMEM((nb, KV_DIM, S_KEYS), BF16),
                        pltpu.VMEM((CONV_K - 1 + DEC_SEQ, nb, CONV_DIM), F32),
                        pltpu.VMEM((POOL_MAX - 1 + DEC_SEQ, nb, POOL_DIM), F32),
                        pltpu.VMEM((max(CONV_DIM, POOL_DIM) // LANES, rows, LANES), F32),
                        pltpu.VMEM((rows, MIX_DIM), F32)],
        compiler_params=pltpu.CompilerParams(dimension_semantics=("arbitrary",),
                                             vmem_limit_bytes=VMEM_LIMIT),
        name="sample_mix_out_ffn",
    )(*operands, *carried, h, ng, wout, wg, wu, wd)


def _block_diag(pool_w):
    eye = jnp.eye(N_POOL_GROUPS, dtype=pool_w.dtype)
    return jnp.einsum('lgcd,gh->lgchd', pool_w, eye).reshape(DEPTH, POOL_DIM, POOL_DIM)


def kernel(x_prompt, x_sample, state_attn_k, state_attn_v, state_conv, state_pool, norm_g, ffn1_wg, ffn1_wu, ffn1_wd, w_in, attn_sinks, conv_dw_w, conv_dw_b, conv_ln_g, conv_ln_b, conv_pw_w, pool_w, pool_scale, w_out, ffn2_wg, ffn2_wu, ffn2_wd):
    hp = x_prompt.reshape(BATCH * SEQ, D_MODEL)
    hs = x_sample.reshape(DEC_BATCH * DEC_SEQ, D_MODEL)
    sk_all = state_attn_k.transpose(0, 1, 3, 4, 2)
    sv_all = state_attn_v.transpose(0, 1, 3, 4, 2)
    sc_all = state_conv.transpose(0, 2, 1, 3)
    sp_all = state_pool.transpose(0, 2, 1, 3)
    f1 = (ffn1_wg, ffn1_wu, ffn1_wd)
    win = w_in
    mixer_w = (attn_sinks, conv_dw_w, conv_dw_b.reshape(DEPTH, 1, CONV_DIM),
               conv_ln_g.reshape(DEPTH, 1, CONV_DIM), conv_ln_b.reshape(DEPTH, 1, CONV_DIM),
               conv_pw_w.astype(BF16), _block_diag(pool_w).astype(BF16),
               pool_scale.reshape(DEPTH, 1, POOL_DIM))
    outs = [[] for _ in range(4)]
    sample_states = [jnp.zeros_like(a) for a in (sk_all, sv_all, sc_all, sp_all)]
    for l in range(DEPTH):
        h1, z, *w2 = _ffn_in(l, hp, hs, norm_g, *f1, win, w_out, ffn2_wg, ffn2_wu, ffn2_wd)
        hp, k1, v1, c1, p1 = _prompt_mix_out_ffn(l, z, h1, *mixer_w, norm_g, *w2)
        hs, *sample_states = _sample_mix_out_ffn(l, z, h1, sk_all, sv_all, sc_all, sp_all, sample_states,
                                                 *mixer_w, norm_g, *w2)
        for lst, val in zip(outs, (k1, v1, c1, p1)):
            lst.append(val)

    kv_p = (DEPTH, BATCH, WINDOW, N_KV_HEADS, HEAD_DIM)
    k2, v2, c2, p2 = sample_states
    return (hp.reshape(BATCH, SEQ, D_MODEL), hs.reshape(DEC_BATCH, DEC_SEQ, D_MODEL),
            jnp.stack(outs[0]).reshape(kv_p), jnp.stack(outs[1]).reshape(kv_p),
            jnp.stack(outs[2]), jnp.stack(outs[3]),
            k2.transpose(0, 1, 4, 2, 3), v2.transpose(0, 1, 4, 2, 3),
            c2.transpose(0, 2, 1, 3), p2.transpose(0, 2, 1, 3))
```

```python
import functools

import numpy as np
import jax
import jax.numpy as jnp
from jax import lax
from jax.experimental import pallas as pl
from jax.experimental.pallas import tpu as pltpu

D_MODEL = 1024
BATCH = 8
SEQ = 2048
DEPTH = 4
DEC_BATCH = 128
DEC_SEQ = 8
PAST_LEN = 8192

N_HEADS = 8
HEAD_DIM = 64
N_KV_HEADS = 2
GROUP = N_HEADS // N_KV_HEADS
ATTN_DIM = N_HEADS * HEAD_DIM
KV_DIM = N_KV_HEADS * HEAD_DIM
WINDOW = 128
CONV_DIM = D_MODEL // 4
CONV_K = 31
POOL_DIM = D_MODEL // 4
POOL_WINDOWS = (2, 4, 8, 16)
N_POOL_GROUPS = 4
POOL_GROUP_DIM = POOL_DIM // N_POOL_GROUPS
POOL_MAX = 16
IN_DIM = ATTN_DIM + 2 * KV_DIM + 2 * CONV_DIM + POOL_DIM
MIX_DIM = ATTN_DIM + CONV_DIM + POOL_DIM
D_FF = ((8 * D_MODEL // 3 + 127) // 128) * 128
RMS_EPS = 1e-6
LN_EPS = 1e-5
NEG_INF = -1e30

Q_OFF = 0
K_OFF = ATTN_DIM
V_OFF = K_OFF + KV_DIM
GA_OFF = V_OFF + KV_DIM
GB_OFF = GA_OFF + CONV_DIM
PU_OFF = GB_OFF + CONV_DIM
CONV_OUT_OFF = ATTN_DIM
POOL_OUT_OFF = ATTN_DIM + CONV_DIM

LANES = 128
SUBLANES = 8
TM_FFN = 512
FFN_SUBTILES = 2
CAST_ROWS = 128
CAST_SLOTS = 4
SIDE_BLOCKS = 32
SIDE_BLOCKS_DOWN = 16
MXU_TILE = 256
FF_SPLITS = (4 * MXU_TILE, 4 * MXU_TILE, 3 * MXU_TILE)
FF_SPLITS_FUSED = (1 * MXU_TILE, 3 * MXU_TILE, 2 * MXU_TILE, 2 * MXU_TILE, 3 * MXU_TILE)
TM_MIX = 512
Q_BLK = 128
CONV_ROWS = 64
CONV_HALO = 32
POOL_HALO = 16
SEQ_BLK = 16
S_KEYS = 256
VMEM_LIMIT = 48 * 1024 * 1024
VMEM_LIMIT_FUSED = 54 * 1024 * 1024

LOG2E = float(np.log2(np.e))
ALIBI_SLOPES2 = [float(s) * LOG2E
                 for s in np.exp2(-8.0 * np.arange(1, N_HEADS + 1, dtype=np.float32) / N_HEADS)]
Q_SCALE2 = HEAD_DIM ** -0.5 * LOG2E

F32 = jnp.float32
BF16 = jnp.bfloat16


def _rms(x, g):
    return x * lax.rsqrt(jnp.mean(x * x, axis=-1, keepdims=True) + RMS_EPS) * g


def _swiglu(xn, wg_ref, wu_ref, wd_ref, splits=(D_FF,), before=(), after=()):
    bounds = np.cumsum((0,) + tuple(splits))
    acc = None
    act = None
    for c in range(len(splits) + 1):
        if c < len(splits):
            if c < len(before) and before[c] is not None:
                before[c]()
            sl = slice(int(bounds[c]), int(bounds[c + 1]))
            g = jnp.dot(xn, wg_ref[:, sl], preferred_element_type=F32)
            u = jnp.dot(xn, wu_ref[:, sl], preferred_element_type=F32)
            nxt = (g * jax.nn.sigmoid(g) * u).astype(BF16)
        if c > 0:
            sl = slice(int(bounds[c - 1]), int(bounds[c]))
            part = jnp.dot(act, wd_ref[sl, :], preferred_element_type=F32)
            acc = part if acc is None else acc + part
            if c - 1 < len(after) and after[c - 1] is not None:
                after[c - 1]()
        act = nxt
    return acc


def _load_weights_as_bf16(layer, pairs, stage_ref, sem_ref):
    slots = stage_ref.shape[0]
    chunks = [(src, dst, r0) for src, dst in pairs for r0 in range(0, dst.shape[0], CAST_ROWS)]

    def chunk_copy(idx):
        src, dst, r0 = chunks[idx]
        return pltpu.make_async_copy(src.at[layer, pl.ds(r0, CAST_ROWS), :],
                                     stage_ref.at[idx % slots, :, 0:dst.shape[1]], sem_ref.at[idx % slots])

    for idx in range(slots - 1):
        chunk_copy(idx).start()
    for idx, (src, dst, r0) in enumerate(chunks):
        if idx + slots - 1 < len(chunks):
            chunk_copy(idx + slots - 1).start()
        chunk_copy(idx).wait()
        dst[r0:r0 + CAST_ROWS, :] = stage_ref[idx % slots, :, 0:dst.shape[1]].astype(BF16)


def _ffn_in_kernel(xp_ref, xs_ref, ng_ref, wg_hbm, wu_hbm, wd_hbm, win_hbm,
                   wout2_ref, wg2_ref, wu2_ref, wd2_ref, h_ref, z_ref, wout2b_ref, wg2b_ref, wu2b_ref, wd2b_ref,
                   wg_ref, wu_ref, wd_ref, win_ref, stage_ref, sem_ref, *, layer):
    @pl.when(pl.program_id(0) == 0)
    def _():
        _load_weights_as_bf16(layer, ((wg_hbm, wg_ref), (wu_hbm, wu_ref), (wd_hbm, wd_ref), (win_hbm, win_ref)),
                              stage_ref, sem_ref)

    for src, dst in ((wout2_ref, wout2b_ref), (wg2_ref, wg2b_ref), (wu2_ref, wu2b_ref), (wd2_ref, wd2b_ref)):
        dst[...] = src[...].astype(BF16)

    from_prompt = pl.program_id(0) < PROMPT_FFN_TILES
    sub = xp_ref.shape[0] // FFN_SUBTILES
    rows = [slice(k * sub, (k + 1) * sub) for k in range(FFN_SUBTILES)]
    xs = [jnp.where(from_prompt, xp_ref[r, :], xs_ref[r, :]) for r in rows]
    xns = [_rms(x, ng_ref[0:1, :]).astype(BF16) for x in xs]
    us = []
    for k in range(FFN_SUBTILES):
        ffn = _swiglu(xns[k], wg_ref, wu_ref, wd_ref, FF_SPLITS)
        h = xs[k] + 0.5 * _rms(ffn, ng_ref[1:2, :])
        h_ref[rows[k], :] = h
        us.append(_rms(h, ng_ref[2:3, :]).astype(BF16))
        if k > 0:
            z_ref[rows[k - 1], :] = jnp.dot(us[k - 1], win_ref[...], preferred_element_type=F32)
    z_ref[rows[-1], :] = jnp.dot(us[-1], win_ref[...], preferred_element_type=F32)


def _out_ffn_kernel(h_ref, mix_ref, ng_ref, wout_ref, wg_ref, wu_ref, wd_ref, y_ref,
                    splits=(D_FF,), before=(), after=()):
    m = jnp.dot(mix_ref[...].astype(BF16), wout_ref[...], preferred_element_type=F32)
    h = h_ref[...] + _rms(m, ng_ref[3:4, :])
    hn = _rms(h, ng_ref[4:5, :]).astype(BF16)
    y_ref[...] = h + 0.5 * _rms(_swiglu(hn, wg_ref, wu_ref, wd_ref, splits, before, after), ng_ref[5:6, :])


def _const_spec(shape):
    return pl.BlockSpec(shape, lambda *_: (0,) * len(shape), pipeline_mode=pl.Buffered(1))


def _layer_spec(l, shape, single_buffer=False):
    index_map = lambda *_: (l,) + (0,) * len(shape)
    if single_buffer:
        return pl.BlockSpec((None,) + shape, index_map, pipeline_mode=pl.Buffered(1))
    return pl.BlockSpec((None,) + shape, index_map)


PROMPT_FFN_TILES = BATCH * SEQ // TM_FFN
SAMPLE_FFN_TILES = DEC_BATCH * DEC_SEQ // TM_FFN


def _ffn_in(l, xp, xs, ng, wg, wu, wd, win, wout2, wg2, wu2, wd2):
    n = xp.shape[0] + xs.shape[0]
    row = lambda i: (i, 0)
    side_in, side_out, side_shape = [], [], []
    for w, blocks in ((wout2, SIDE_BLOCKS), (wg2, SIDE_BLOCKS), (wu2, SIDE_BLOCKS), (wd2, SIDE_BLOCKS_DOWN)):
        k_dim, n_dim = w.shape[1:]
        blk = (k_dim // blocks, n_dim)
        side_in.append(pl.BlockSpec((None,) + blk, lambda i, b=blocks: (l, jnp.minimum(i, b - 1), 0)))
        side_out.append(pl.BlockSpec(blk, lambda i, b=blocks: (jnp.minimum(i, b - 1), 0)))
        side_shape.append(jax.ShapeDtypeStruct((k_dim, n_dim), BF16))
    return pl.pallas_call(
        functools.partial(_ffn_in_kernel, layer=l),
        grid=(PROMPT_FFN_TILES + SAMPLE_FFN_TILES,),
        in_specs=[pl.BlockSpec((TM_FFN, D_MODEL), lambda i: (jnp.minimum(i, PROMPT_FFN_TILES - 1), 0)),
                  pl.BlockSpec((TM_FFN, D_MODEL), lambda i: (jnp.maximum(i - PROMPT_FFN_TILES, 0), 0)),
                  _layer_spec(l, (6, D_MODEL), True)] + [pl.BlockSpec(memory_space=pl.ANY)] * 4 + side_in,
        out_specs=[pl.BlockSpec((TM_FFN, D_MODEL), row), pl.BlockSpec((TM_FFN, IN_DIM), row)] + side_out,
        out_shape=[jax.ShapeDtypeStruct((n, D_MODEL), F32), jax.ShapeDtypeStruct((n, IN_DIM), F32)] + side_shape,
        scratch_shapes=[pltpu.VMEM((D_MODEL, D_FF), BF16), pltpu.VMEM((D_MODEL, D_FF), BF16),
                        pltpu.VMEM((D_FF, D_MODEL), BF16), pltpu.VMEM((D_MODEL, IN_DIM), BF16),
                        pltpu.VMEM((CAST_SLOTS, CAST_ROWS, D_FF), F32), pltpu.SemaphoreType.DMA((CAST_SLOTS,))],
        compiler_params=pltpu.CompilerParams(dimension_semantics=("arbitrary",),
                                             vmem_limit_bytes=VMEM_LIMIT_FUSED),
        name="ffn_in",
    )(xp, xs, ng, wg, wu, wd, win, wout2, wg2, wu2, wd2)


def _stack_q_heads(q):
    lane = lax.broadcasted_iota(jnp.int32, q.shape[:-1] + (LANES,), q.ndim - 1)
    pieces = []
    for h in range(N_HEADS):
        kvh = h // GROUP
        slab = q[..., (h // 2) * LANES:(h // 2 + 1) * LANES]
        if h % 2 != kvh:
            slab = pltpu.roll(slab, HEAD_DIM, q.ndim - 1)
        keep = (lane >= kvh * HEAD_DIM) & (lane < (kvh + 1) * HEAD_DIM)
        pieces.append((jnp.where(keep, slab, 0.0) * Q_SCALE2).astype(BF16))
    return jnp.concatenate(pieces, axis=-2)


def _unstack_heads(o_heads):
    ndim = o_heads[0].ndim
    lane = lax.broadcasted_iota(jnp.int32, o_heads[0].shape, ndim - 1)
    slabs = []
    for j in range(N_HEADS // 2):
        a, b = o_heads[2 * j], o_heads[2 * j + 1]
        kvh = (2 * j) // GROUP
        if kvh == 0:
            b = pltpu.roll(b, HEAD_DIM, ndim - 1)
        else:
            a = pltpu.roll(a, HEAD_DIM, ndim - 1)
        slabs.append(jnp.where(lane < HEAD_DIM, a, b))
    return jnp.concatenate(slabs, axis=-1)


def _alibi_bias(dist_f):
    return [ALIBI_SLOPES2[h] * dist_f for h in range(N_HEADS)]


def _softmax_heads(s, valid, bias, sinks, rows):
    es, rinv = [], []
    for h in range(N_HEADS):
        sh = s[..., h * rows:(h + 1) * rows, :]
        l = jnp.where(valid, sh - bias[h], NEG_INF)
        m = jnp.maximum(jnp.max(l, axis=-1, keepdims=True), sinks[h])
        e = jnp.exp2(l - m)
        den = jnp.sum(e, axis=-1, keepdims=True) + jnp.exp2(sinks[h] - m)
        es.append(e.astype(BF16))
        rinv.append(1.0 / den)
    return es, rinv


def _conv_ln_act(c, lng_ref, lnb_ref):
    mu = jnp.mean(c, axis=-1, keepdims=True)
    xc = c - mu
    y = xc * lax.rsqrt(jnp.mean(xc * xc, axis=-1, keepdims=True) + LN_EPS) * lng_ref[...] + lnb_ref[...]
    return (y * jax.nn.sigmoid(y)).astype(BF16)


def _pool_diff(sums, cnt, cur):
    lane = lax.broadcasted_iota(jnp.int32, cur.shape, cur.ndim - 1)
    wsum = sums[-1]
    for g in range(N_POOL_GROUPS - 2, -1, -1):
        wsum = jnp.where(lane < (g + 1) * POOL_GROUP_DIM, sums[g], wsum)
    return (wsum / cnt - cur).astype(BF16)


def _pool_window_lane():
    lane = lax.broadcasted_iota(jnp.int32, (1, POOL_DIM), 1)
    w = jnp.full((1, POOL_DIM), POOL_WINDOWS[-1], jnp.int32)
    for g in range(N_POOL_GROUPS - 2, -1, -1):
        w = jnp.where(lane < (g + 1) * POOL_GROUP_DIM, POOL_WINDOWS[g], w)
    return w


def _dw_conv_rows(gx_ref, dww_ref, r0, rows):
    first = CONV_HALO - (CONV_K - 1)
    outs = []
    for c0 in range(0, CONV_DIM, LANES):
        cols = slice(c0, c0 + LANES)
        out = None
        for r in range(SUBLANES):
            ext = rows if r == 0 else rows + SUBLANES
            acc = None
            for m in range((first + CONV_K - 1) // SUBLANES + 1):
                j = SUBLANES * m + r - first
                if j < 0 or j >= CONV_K:
                    continue
                lo = r0 + SUBLANES * m
                term = dww_ref[j:j + 1, cols] * gx_ref[lo:lo + ext, cols]
                acc = term if acc is None else acc + term
            part = acc if r == 0 else acc[r:r + rows, :]
            out = part if out is None else out + part
        outs.append(out)
    return jnp.concatenate(outs, axis=1)


def _prompt_mixer_pieces(i, sinks_ref, z_ref, dww_ref, dwb_ref, lng_ref, lnb_ref, pw_ref, poolw_ref, pools_ref,
                         mix_ref, nk_ref, nv_ref, nc_ref, np_ref,
                         kx_ref, vx_ref, gx_ref, px_ref, layer):
    tm = TM_MIX

    kx_ref[WINDOW:, :] = z_ref[:, K_OFF:K_OFF + KV_DIM].astype(BF16)
    vx_ref[WINDOW:, :] = z_ref[:, V_OFF:V_OFF + KV_DIM].astype(BF16)
    gx_ref[CONV_HALO:, :] = z_ref[:, GA_OFF:GA_OFF + CONV_DIM] * jax.nn.sigmoid(z_ref[:, GB_OFF:GB_OFF + CONV_DIM])
    px_ref[POOL_HALO:, :] = z_ref[:, PU_OFF:PU_OFF + POOL_DIM]

    q_io = lax.broadcasted_iota(jnp.int32, (Q_BLK, WINDOW + Q_BLK), 0)
    s_io = lax.broadcasted_iota(jnp.int32, (Q_BLK, WINDOW + Q_BLK), 1)
    dist = q_io + WINDOW - s_io
    band = (dist >= 0) & (dist < WINDOW)
    bias = _alibi_bias(dist.astype(F32))
    sinks = [sinks_ref[layer, h] * LOG2E for h in range(N_HEADS)]
    w_lane = _pool_window_lane()

    def head(blk):
        r0 = blk * Q_BLK
        first_key_pos = i * tm + r0 - WINDOW
        valid = band & (s_io + first_key_pos >= 0)
        qs = _stack_q_heads(z_ref[r0:r0 + Q_BLK, Q_OFF:Q_OFF + ATTN_DIM])
        kb = kx_ref[r0:r0 + WINDOW + Q_BLK, :]
        s = lax.dot_general(qs, kb, (((1,), (1,)), ((), ())), preferred_element_type=F32)
        es, rinv = _softmax_heads(s, valid, bias, sinks, Q_BLK)
        ys = []
        for rc in range(Q_BLK // CONV_ROWS):
            c = _dw_conv_rows(gx_ref, dww_ref, r0 + rc * CONV_ROWS, CONV_ROWS) + dwb_ref[...]
            ys.append(_conv_ln_act(c, lng_ref, lnb_ref))
        run = px_ref[r0:r0 + POOL_HALO + Q_BLK, :]
        cur = run[POOL_HALO:, :]
        sums = []
        span = 1
        while span < POOL_MAX:
            run = run + pltpu.roll(run, span, 0)
            span *= 2
            if span in POOL_WINDOWS:
                sums.append(run[POOL_HALO:, :])
        pos = i * tm + r0 + lax.broadcasted_iota(jnp.int32, (Q_BLK, 1), 0)
        cnt = jnp.minimum(w_lane, pos + 1).astype(F32)
        return (jnp.concatenate(es, axis=0), rinv, jnp.concatenate(ys, axis=0), _pool_diff(sums, cnt, cur))

    def tail(blk, p, rinv, y, d):
        r0 = blk * Q_BLK
        o = jnp.dot(p, vx_ref[r0:r0 + WINDOW + Q_BLK, :], preferred_element_type=F32)
        o_heads = [o[h * Q_BLK:(h + 1) * Q_BLK, :] * rinv[h] for h in range(N_HEADS)]
        mix_ref[r0:r0 + Q_BLK, 0:ATTN_DIM] = _unstack_heads(o_heads).astype(mix_ref.dtype)
        conv = jnp.dot(y, pw_ref[...], preferred_element_type=F32)
        mix_ref[r0:r0 + Q_BLK, CONV_OUT_OFF:CONV_OUT_OFF + CONV_DIM] = conv.astype(mix_ref.dtype)
        pool = jnp.dot(d, poolw_ref[...], preferred_element_type=F32) * pools_ref[...]
        mix_ref[r0:r0 + Q_BLK, POOL_OUT_OFF:POOL_OUT_OFF + POOL_DIM] = pool.astype(mix_ref.dtype)

    def finish():
        nk_ref[...] = z_ref[tm - WINDOW:, K_OFF:K_OFF + KV_DIM]
        nv_ref[...] = z_ref[tm - WINDOW:, V_OFF:V_OFF + KV_DIM]
        nc_ref[...] = gx_ref[CONV_HALO + tm - (CONV_K - 1):, :]
        np_ref[...] = px_ref[POOL_HALO + tm - (POOL_MAX - 1):, :]
        kx_ref[0:WINDOW, :] = kx_ref[tm:tm + WINDOW, :]
        vx_ref[0:WINDOW, :] = vx_ref[tm:tm + WINDOW, :]
        gx_ref[0:CONV_HALO, :] = gx_ref[tm:tm + CONV_HALO, :]
        px_ref[0:POOL_HALO, :] = px_ref[tm:tm + POOL_HALO, :]

    pending = {}

    def run_head(blk):
        pending[blk] = head(blk)

    def run_tail(blk):
        tail(blk, *pending.pop(blk))

    blocks = range(tm // Q_BLK)
    return ([functools.partial(run_head, blk) for blk in blocks],
            [functools.partial(run_tail, blk) for blk in blocks], finish)


def _mixer_weight_specs(l):
    return [pl.BlockSpec(memory_space=pltpu.SMEM),
            _layer_spec(l, (CONV_K, CONV_DIM)), _layer_spec(l, (1, CONV_DIM)),
            _layer_spec(l, (1, CONV_DIM)), _layer_spec(l, (1, CONV_DIM)),
            _layer_spec(l, (CONV_DIM, CONV_DIM)), _layer_spec(l, (POOL_DIM, POOL_DIM)),
            _layer_spec(l, (1, POOL_DIM))]


PROMPT_TILES = BATCH * SEQ // TM_MIX
TILES_PER_SEQ = SEQ // TM_MIX


def _prompt_mix_out_ffn_kernel(sinks_ref, z_ref, h_ref, dww_ref, dwb_ref, lng_ref, lnb_ref, pw_ref, poolw_ref,
                               pools_ref, ng_ref, wout_ref, wg_ref, wu_ref, wd_ref,
                               y_ref, nk_ref, nv_ref, nc_ref, np_ref,
                               kx_ref, vx_ref, gx_ref, px_ref, mix_ref, h2_ref, hn_ref, *, layer):
    s = pl.program_id(0)
    i = lax.rem(jnp.minimum(s, PROMPT_TILES - 1), TILES_PER_SEQ)

    @pl.when(i == 0)
    def _():
        kx_ref[0:WINDOW, :] = jnp.zeros((WINDOW, KV_DIM), BF16)
        vx_ref[0:WINDOW, :] = jnp.zeros((WINDOW, KV_DIM), BF16)
        gx_ref[0:CONV_HALO, :] = jnp.zeros((CONV_HALO, CONV_DIM), F32)
        px_ref[0:POOL_HALO, :] = jnp.zeros((POOL_HALO, POOL_DIM), F32)

    @pl.when(s == 0)
    def _():
        h2_ref[...] = jnp.zeros(h2_ref.shape, h2_ref.dtype)
        hn_ref[...] = jnp.zeros(hn_ref.shape, hn_ref.dtype)

    heads, tails, finish = _prompt_mixer_pieces(i, sinks_ref, z_ref, dww_ref, dwb_ref, lng_ref, lnb_ref, pw_ref,
                                                poolw_ref, pools_ref, mix_ref, nk_ref, nv_ref, nc_ref, np_ref,
                                                kx_ref, vx_ref, gx_ref, px_ref, layer)
    ffn = _swiglu(hn_ref[...], wg_ref, wu_ref, wd_ref, FF_SPLITS_FUSED, [None] + heads, tails)
    finish()
    y_ref[...] = h2_ref[...] + 0.5 * _rms(ffn, ng_ref[5:6, :])
    m = jnp.dot(mix_ref[...], wout_ref[...], preferred_element_type=F32)
    h2 = h_ref[...] + _rms(m, ng_ref[3:4, :])
    h2_ref[...] = h2
    hn_ref[...] = _rms(h2, ng_ref[4:5, :]).astype(BF16)


def _prompt_mix_out_ffn(l, z, h, sinks, dww, dwb, lng, lnb, pw, poolw, pools, ng, wout, wg, wu, wd):
    last = PROMPT_TILES - 1
    cur = lambda s: (jnp.minimum(s, last), 0)
    prev = lambda s: (jnp.maximum(s - 1, 0), 0)
    seq3 = lambda s: (jnp.minimum(s, last) // TILES_PER_SEQ, 0, 0)
    wspecs = _mixer_weight_specs(l)
    return pl.pallas_call(
        functools.partial(_prompt_mix_out_ffn_kernel, layer=l),
        grid=(PROMPT_TILES + 1,),
        in_specs=[wspecs[0], pl.BlockSpec((TM_MIX, IN_DIM), cur), pl.BlockSpec((TM_MIX, D_MODEL), cur)]
                 + wspecs[1:]
                 + [_layer_spec(l, (6, D_MODEL), True), _const_spec((MIX_DIM, D_MODEL)),
                    _const_spec((D_MODEL, D_FF)), _const_spec((D_MODEL, D_FF)),
                    _const_spec((D_FF, D_MODEL))],
        out_specs=[pl.BlockSpec((TM_MIX, D_MODEL), prev),
                   pl.BlockSpec((None, WINDOW, KV_DIM), seq3), pl.BlockSpec((None, WINDOW, KV_DIM), seq3),
                   pl.BlockSpec((None, CONV_K - 1, CONV_DIM), seq3),
                   pl.BlockSpec((None, POOL_MAX - 1, POOL_DIM), seq3)],
        out_shape=[jax.ShapeDtypeStruct((BATCH * SEQ, D_MODEL), F32),
                   jax.ShapeDtypeStruct((BATCH, WINDOW, KV_DIM), F32),
                   jax.ShapeDtypeStruct((BATCH, WINDOW, KV_DIM), F32),
                   jax.ShapeDtypeStruct((BATCH, CONV_K - 1, CONV_DIM), F32),
                   jax.ShapeDtypeStruct((BATCH, POOL_MAX - 1, POOL_DIM), F32)],
        scratch_shapes=[pltpu.VMEM((WINDOW + TM_MIX, KV_DIM), BF16),
                        pltpu.VMEM((WINDOW + TM_MIX, KV_DIM), BF16),
                        pltpu.VMEM((CONV_HALO + TM_MIX, CONV_DIM), F32),
                        pltpu.VMEM((POOL_HALO + TM_MIX, POOL_DIM), F32),
                        pltpu.VMEM((TM_MIX, MIX_DIM), BF16),
                        pltpu.VMEM((TM_MIX, D_MODEL), F32),
                        pltpu.VMEM((TM_MIX, D_MODEL), BF16)],
        compiler_params=pltpu.CompilerParams(dimension_semantics=("arbitrary",),
                                             vmem_limit_bytes=VMEM_LIMIT_FUSED),
        name="prompt_mix_out_ffn",
    )(sinks, z, h, dww, dwb, lng, lnb, pw, poolw, pools, ng, wout, wg, wu, wd)


def _sample_mixer_kernel(sinks_ref, z3_ref, z2_ref, sk_ref, sv_ref, sc_ref, sp_ref,
                         dww_ref, dwb_ref, lng_ref, lnb_ref, pw_ref, poolw_ref, pools_ref,
                         ck_ref, cv_ref, cc_ref, cp_ref,
                         mix_ref, nk_ref, nv_ref, nc_ref, np_ref,
                         kx_ref, vx_ref, gx_ref, px_ref, re_ref, *, layer):
    del ck_ref, cv_ref, cc_ref, cp_ref
    nb, t = SEQ_BLK, DEC_SEQ
    lane = lax.broadcasted_iota(jnp.int32, (KV_DIM, LANES), 1)

    for x_ref, s_ref, n_ref, off in ((kx_ref, sk_ref, nk_ref, K_OFF), (vx_ref, sv_ref, nv_ref, V_OFF)):
        new_t = z2_ref[:, off:off + KV_DIM].T
        for b in range(nb):
            old = jnp.concatenate([s_ref[b, kv] for kv in range(N_KV_HEADS)], axis=0)
            x_ref[b, :, 0:WINDOW] = old.astype(BF16)
            fresh = pltpu.roll(new_t, (LANES - b * t) % LANES, 1)
            x_ref[b, :, WINDOW:] = jnp.where(lane < t, fresh, 0.0).astype(BF16)
            shifted = jnp.where(lane < WINDOW - t, pltpu.roll(old, WINDOW - t, 1),
                                pltpu.roll(new_t, (WINDOW - t - b * t) % LANES, 1))
            for kv in range(N_KV_HEADS):
                n_ref[b, kv] = shifted[kv * HEAD_DIM:(kv + 1) * HEAD_DIM, :]

    q_io = lax.broadcasted_iota(jnp.int32, (t, S_KEYS), 0)
    s_io = lax.broadcasted_iota(jnp.int32, (t, S_KEYS), 1)
    dist = q_io + WINDOW - s_io
    valid = (dist >= 0) & (dist < WINDOW) & (s_io + (PAST_LEN - WINDOW) >= 0)
    bias = _alibi_bias(dist.astype(F32))
    sinks = [sinks_ref[layer, h] * LOG2E for h in range(N_HEADS)]

    qs = _stack_q_heads(z3_ref[:, :, Q_OFF:Q_OFF + ATTN_DIM])
    s = jnp.einsum('bqd,bdk->bqk', qs, kx_ref[...], preferred_element_type=F32)
    es, rinv = _softmax_heads(s, valid, bias, sinks, t)
    o = jnp.einsum('bqk,bdk->bqd', jnp.concatenate(es, axis=1), vx_ref[...], preferred_element_type=F32)
    o_heads = [o[:, h * t:(h + 1) * t, :] * rinv[h] for h in range(N_HEADS)]
    attn = _unstack_heads(o_heads)
    mix_ref[:, 0:ATTN_DIM] = attn.reshape(nb * t, ATTN_DIM)

    def to_token_major(x, dst_ref, first):
        for cb in range(x.shape[1] // LANES):
            re_ref[cb] = x[:, cb * LANES:(cb + 1) * LANES]
        for tok in range(t):
            for cb in range(x.shape[1] // LANES):
                dst_ref[first + tok, :, cb * LANES:(cb + 1) * LANES] = re_ref[cb, pl.ds(tok, nb, stride=t), :]

    def to_seq_major(x):
        for tok in range(t):
            for cb in range(x.shape[1] // LANES):
                re_ref[cb, pl.ds(tok, nb, stride=t), :] = x[tok * nb:(tok + 1) * nb, cb * LANES:(cb + 1) * LANES]
        return jnp.concatenate([re_ref[cb] for cb in range(x.shape[1] // LANES)], axis=1)

    hist = CONV_K - 1
    gx_ref[0:hist] = sc_ref[...]
    to_token_major(z2_ref[:, GA_OFF:GA_OFF + CONV_DIM] * jax.nn.sigmoid(z2_ref[:, GB_OFF:GB_OFF + CONV_DIM]),
                   gx_ref, hist)
    nc_ref[...] = gx_ref[t:t + hist]
    ys = []
    for tok in range(t):
        c = jnp.broadcast_to(dwb_ref[...], (nb, CONV_DIM))
        for j in range(CONV_K):
            c = c + dww_ref[j:j + 1, :] * gx_ref[tok + j]
        ys.append(_conv_ln_act(c, lng_ref, lnb_ref))
    conv = jnp.dot(jnp.concatenate(ys, axis=0), pw_ref[...], preferred_element_type=F32)
    mix_ref[:, CONV_OUT_OFF:CONV_OUT_OFF + CONV_DIM] = to_seq_major(conv)

    hist = POOL_MAX - 1
    px_ref[0:hist] = sp_ref[...]
    to_token_major(z2_ref[:, PU_OFF:PU_OFF + POOL_DIM], px_ref, hist)
    np_ref[...] = px_ref[t:t + hist]
    w_lane = _pool_window_lane()
    ds = []
    for tok in range(t):
        cur = px_ref[hist + tok]
        run = cur
        sums = []
        for back in range(1, POOL_MAX):
            run = run + px_ref[hist + tok - back]
            if back + 1 in POOL_WINDOWS:
                sums.append(run)
        cnt = jnp.minimum(w_lane, PAST_LEN + tok + 1).astype(F32)
        ds.append(_pool_diff(sums, cnt, cur))
    pool = jnp.dot(jnp.concatenate(ds, axis=0), poolw_ref[...], preferred_element_type=F32) * pools_ref[...]
    mix_ref[:, POOL_OUT_OFF:POOL_OUT_OFF + POOL_DIM] = to_seq_major(pool)


def _sample_mix_out_ffn_kernel(sinks_ref, z3_ref, z2_ref, sk_ref, sv_ref, sc_ref, sp_ref,
                               dww_ref, dwb_ref, lng_ref, lnb_ref, pw_ref, poolw_ref, pools_ref,
                               ck_ref, cv_ref, cc_ref, cp_ref, h_ref, ng_ref, wout_ref, wg_ref, wu_ref, wd_ref,
                               y_ref, nk_ref, nv_ref, nc_ref, np_ref,
                               kx_ref, vx_ref, gx_ref, px_ref, re_ref, mix_ref, *, layer):
    _sample_mixer_kernel(sinks_ref, z3_ref, z2_ref, sk_ref, sv_ref, sc_ref, sp_ref,
                         dww_ref, dwb_ref, lng_ref, lnb_ref, pw_ref, poolw_ref, pools_ref,
                         ck_ref, cv_ref, cc_ref, cp_ref, mix_ref, nk_ref, nv_ref, nc_ref, np_ref,
                         kx_ref, vx_ref, gx_ref, px_ref, re_ref, layer=layer)
    _out_ffn_kernel(h_ref, mix_ref, ng_ref, wout_ref, wg_ref, wu_ref, wd_ref, y_ref, FF_SPLITS)


def _sample_mix_out_ffn(l, z, h, sk, sv, sc, sp, new_states, sinks, dww, dwb, lng, lnb, pw, poolw, pools,
                        ng, wout, wg, wu, wd):
    nb = SEQ_BLK
    rows = nb * DEC_SEQ
    assert rows == LANES and S_KEYS == 2 * LANES and WINDOW == LANES
    kv_spec = pl.BlockSpec((None, nb, N_KV_HEADS, HEAD_DIM, WINDOW), lambda j: (l, j, 0, 0, 0))
    rows_spec = lambda r, ch: pl.BlockSpec((None, r, nb, ch), lambda j: (l, 0, j, 0))
    wspecs = _mixer_weight_specs(l)
    first = BATCH * SEQ // rows
    operands = (sinks, z.reshape(-1, DEC_SEQ, IN_DIM), z, sk, sv, sc, sp, dww, dwb, lng, lnb, pw, poolw, pools)
    carried = tuple(new_states)
    return pl.pallas_call(
        functools.partial(_sample_mix_out_ffn_kernel, layer=l),
        grid=(DEC_BATCH // nb,),
        in_specs=[wspecs[0],
                  pl.BlockSpec((nb, DEC_SEQ, IN_DIM), lambda j: (first + j, 0, 0)),
                  pl.BlockSpec((rows, IN_DIM), lambda j: (first + j, 0)),
                  kv_spec, kv_spec, rows_spec(CONV_K - 1, CONV_DIM), rows_spec(POOL_MAX - 1, POOL_DIM)]
                 + wspecs[1:] + [pl.BlockSpec(memory_space=pl.ANY)] * len(carried)
                 + [pl.BlockSpec((rows, D_MODEL), lambda j: (first + j, 0)),
                    _layer_spec(l, (6, D_MODEL), True), _const_spec((MIX_DIM, D_MODEL)),
                    _const_spec((D_MODEL, D_FF)), _const_spec((D_MODEL, D_FF)),
                    _const_spec((D_FF, D_MODEL))],
        out_specs=[pl.BlockSpec((rows, D_MODEL), lambda j: (j, 0)),
                   kv_spec, kv_spec, rows_spec(CONV_K - 1, CONV_DIM), rows_spec(POOL_MAX - 1, POOL_DIM)],
        out_shape=[jax.ShapeDtypeStruct((DEC_BATCH * DEC_SEQ, D_MODEL), F32),
                   jax.ShapeDtypeStruct(sk.shape, F32), jax.ShapeDtypeStruct(sv.shape, F32),
                   jax.ShapeDtypeStruct(sc.shape, F32), jax.ShapeDtypeStruct(sp.shape, F32)],
        input_output_aliases={len(operands) + n: 1 + n for n in range(len(carried))},
        scratch_shapes=[pltpu.VMEM((nb, KV_DIM, S_KEYS), BF16),
                        pltpu.VMEM((nb, KV_DIM, S_KEYS), BF16),
                        pltpu.VMEM((CONV_K - 1 + DEC_SEQ, nb, CONV_DIM), F32),
                        pltpu.VMEM((POOL_MAX - 1 + DEC_SEQ, nb, POOL_DIM), F32),
                        pltpu.VMEM((max(CONV_DIM, POOL_DIM) // LANES, rows, LANES), F32),
                        pltpu.VMEM((rows, MIX_DIM), F32)],
        compiler_params=pltpu.CompilerParams(dimension_semantics=("arbitrary",),
                                             vmem_limit_bytes=VMEM_LIMIT),
        name="sample_mix_out_ffn",
    )(*operands, *carried, h, ng, wout, wg, wu, wd)


def _block_diag(pool_w):
    eye = jnp.eye(N_POOL_GROUPS, dtype=pool_w.dtype)
    return jnp.einsum('lgcd,gh->lgchd', pool_w, eye).reshape(DEPTH, POOL_DIM, POOL_DIM)


def kernel(x_prompt, x_sample, state_attn_k, state_attn_v, state_conv, state_pool, norm_g, ffn1_wg, ffn1_wu, ffn1_wd, w_in, attn_sinks, conv_dw_w, conv_dw_b, conv_ln_g, conv_ln_b, conv_pw_w, pool_w, pool_scale, w_out, ffn2_wg, ffn2_wu, ffn2_wd):
    hp = x_prompt.reshape(BATCH * SEQ, D_MODEL)
    hs = x_sample.reshape(DEC_BATCH * DEC_SEQ, D_MODEL)
    sk_all = state_attn_k.transpose(0, 1, 3, 4, 2)
    sv_all = state_attn_v.transpose(0, 1, 3, 4, 2)
    sc_all = state_conv.transpose(0, 2, 1, 3)
    sp_all = state_pool.transpose(0, 2, 1, 3)
    f1 = (ffn1_wg, ffn1_wu, ffn1_wd)
    win = w_in
    mixer_w = (attn_sinks, conv_dw_w, conv_dw_b.reshape(DEPTH, 1, CONV_DIM),
               conv_ln_g.reshape(DEPTH, 1, CONV_DIM), conv_ln_b.reshape(DEPTH, 1, CONV_DIM),
               conv_pw_w.astype(BF16), _block_diag(pool_w).astype(BF16),
               pool_scale.reshape(DEPTH, 1, POOL_DIM))
    outs = [[] for _ in range(4)]
    sample_states = [jnp.zeros_like(a) for a in (sk_all, sv_all, sc_all, sp_all)]
    for l in range(DEPTH):
        h1, z, *w2 = _ffn_in(l, hp, hs, norm_g, *f1, win, w_out, ffn2_wg, ffn2_wu, ffn2_wd)
        hp, k1, v1, c1, p1 = _prompt_mix_out_ffn(l, z, h1, *mixer_w, norm_g, *w2)
        hs, *sample_states = _sample_mix_out_ffn(l, z, h1, sk_all, sv_all, sc_all, sp_all, sample_states,
                                                 *mixer_w, norm_g, *w2)
        for lst, val in zip(outs, (k1, v1, c1, p1)):
            lst.append(val)

    kv_p = (DEPTH, BATCH, WINDOW, N_KV_HEADS, HEAD_DIM)
    k2, v2, c2, p2 = sample_states
    return (hp.reshape(BATCH, SEQ, D_MODEL), hs.reshape(DEC_BATCH, DEC_SEQ, D_MODEL),
            jnp.stack(outs[0]).reshape(kv_p), jnp.stack(outs[1]).reshape(kv_p),
            jnp.stack(outs[2]), jnp.stack(outs[3]),
            k2.transpose(0, 1, 4, 2, 3), v2.transpose(0, 1, 4, 2, 3),
            c2.transpose(0, 2, 1, 3), p2.transpose(0, 2, 1, 3))
```

```python
import functools

import numpy as np
import jax
import jax.numpy as jnp
from jax import lax
from jax.experimental import pallas as pl
from jax.experimental.pallas import tpu as pltpu

D_MODEL = 1024
BATCH = 8
SEQ = 2048
DEPTH = 4
DEC_BATCH = 128
DEC_SEQ = 8
PAST_LEN = 8192

N_HEADS = 8
HEAD_DIM = 64
N_KV_HEADS = 2
GROUP = N_HEADS // N_KV_HEADS
ATTN_DIM = N_HEADS * HEAD_DIM
KV_DIM = N_KV_HEADS * HEAD_DIM
WINDOW = 128
CONV_DIM = D_MODEL // 4
CONV_K = 31
POOL_DIM = D_MODEL // 4
POOL_WINDOWS = (2, 4, 8, 16)
N_POOL_GROUPS = 4
POOL_GROUP_DIM = POOL_DIM // N_POOL_GROUPS
POOL_MAX = 16
IN_DIM = ATTN_DIM + 2 * KV_DIM + 2 * CONV_DIM + POOL_DIM
MIX_DIM = ATTN_DIM + CONV_DIM + POOL_DIM
D_FF = ((8 * D_MODEL // 3 + 127) // 128) * 128
RMS_EPS = 1e-6
LN_EPS = 1e-5
NEG_INF = -1e30

Q_OFF = 0
K_OFF = ATTN_DIM
V_OFF = K_OFF + KV_DIM
GA_OFF = V_OFF + KV_DIM
GB_OFF = GA_OFF + CONV_DIM
PU_OFF = GB_OFF + CONV_DIM
CONV_OUT_OFF = ATTN_DIM
POOL_OUT_OFF = ATTN_DIM + CONV_DIM

LANES = 128
SUBLANES = 8
TM_FFN = 512
FFN_SUBTILES = 2
CAST_ROWS = 128
CAST_SLOTS = 4
SIDE_BLOCKS = 32
SIDE_BLOCKS_DOWN = 16
MXU_TILE = 256
FF_SPLITS = (4 * MXU_TILE, 4 * MXU_TILE, 3 * MXU_TILE)
FF_SPLITS_FUSED = (1 * MXU_TILE, 3 * MXU_TILE, 2 * MXU_TILE, 2 * MXU_TILE, 3 * MXU_TILE)
TM_MIX = 512
Q_BLK = 128
CONV_ROWS = 64
CONV_HALO = 32
POOL_HALO = 16
SEQ_BLK = 16
S_KEYS = 256
VMEM_LIMIT = 48 * 1024 * 1024
VMEM_LIMIT_FUSED = 54 * 1024 * 1024

LOG2E = float(np.log2(np.e))
ALIBI_SLOPES2 = [float(s) * LOG2E
                 for s in np.exp2(-8.0 * np.arange(1, N_HEADS + 1, dtype=np.float32) / N_HEADS)]
Q_SCALE2 = HEAD_DIM ** -0.5 * LOG2E

F32 = jnp.float32
BF16 = jnp.bfloat16


def _rms(x, g):
    return x * lax.rsqrt(jnp.mean(x * x, axis=-1, keepdims=True) + RMS_EPS) * g


def _swiglu(xn, wg_ref, wu_ref, wd_ref, splits=(D_FF,), before=(), after=()):
    bounds = np.cumsum((0,) + tuple(splits))
    acc = None
    act = None
    for c in range(len(splits) + 1):
        if c < len(splits):
            if c < len(before) and before[c] is not None:
                before[c]()
            sl = slice(int(bounds[c]), int(bounds[c + 1]))
            g = jnp.dot(xn, wg_ref[:, sl], preferred_element_type=F32)
            u = jnp.dot(xn, wu_ref[:, sl], preferred_element_type=F32)
            nxt = (g * jax.nn.sigmoid(g) * u).astype(BF16)
        if c > 0:
            sl = slice(int(bounds[c - 1]), int(bounds[c]))
            part = jnp.dot(act, wd_ref[sl, :], preferred_element_type=F32)
            acc = part if acc is None else acc + part
            if c - 1 < len(after) and after[c - 1] is not None:
                after[c - 1]()
        act = nxt
    return acc


def _load_weights_as_bf16(layer, pairs, stage_ref, sem_ref):
    slots = stage_ref.shape[0]
    chunks = [(src, dst, r0) for src, dst in pairs for r0 in range(0, dst.shape[0], CAST_ROWS)]

    def chunk_copy(idx):
        src, dst, r0 = chunks[idx]
        return pltpu.make_async_copy(src.at[layer, pl.ds(r0, CAST_ROWS), :],
                                     stage_ref.at[idx % slots, :, 0:dst.shape[1]], sem_ref.at[idx % slots])

    for idx in range(slots - 1):
        chunk_copy(idx).start(priority=idx % 2)
    for idx, (src, dst, r0) in enumerate(chunks):
        if idx + slots - 1 < len(chunks):
            chunk_copy(idx + slots - 1).start(priority=(idx + slots - 1) % 2)
        chunk_copy(idx).wait()
        dst[r0:r0 + CAST_ROWS, :] = stage_ref[idx % slots, :, 0:dst.shape[1]].astype(BF16)


def _ffn_in_kernel(xp_ref, xs_ref, ng_ref, wg_hbm, wu_hbm, wd_hbm, win_hbm,
                   wout2_ref, wg2_ref, wu2_ref, wd2_ref, h_ref, z_ref, wout2b_ref, wg2b_ref, wu2b_ref, wd2b_ref,
                   wg_ref, wu_ref, wd_ref, win_ref, stage_ref, sem_ref, *, layer):
    @pl.when(pl.program_id(0) == 0)
    def _():
        _load_weights_as_bf16(layer, ((wg_hbm, wg_ref), (wu_hbm, wu_ref), (wd_hbm, wd_ref), (win_hbm, win_ref)),
                              stage_ref, sem_ref)

    for src, dst in ((wout2_ref, wout2b_ref), (wg2_ref, wg2b_ref), (wu2_ref, wu2b_ref), (wd2_ref, wd2b_ref)):
        dst[...] = src[...].astype(BF16)

    from_prompt = pl.program_id(0) < PROMPT_FFN_TILES
    sub = xp_ref.shape[0] // FFN_SUBTILES
    rows = [slice(k * sub, (k + 1) * sub) for k in range(FFN_SUBTILES)]
    xs = [jnp.where(from_prompt, xp_ref[r, :], xs_ref[r, :]) for r in rows]
    xns = [_rms(x, ng_ref[0:1, :]).astype(BF16) for x in xs]
    us = []
    for k in range(FFN_SUBTILES):
        ffn = _swiglu(xns[k], wg_ref, wu_ref, wd_ref, FF_SPLITS)
        h = xs[k] + 0.5 * _rms(ffn, ng_ref[1:2, :])
        h_ref[rows[k], :] = h
        us.append(_rms(h, ng_ref[2:3, :]).astype(BF16))
        if k > 0:
            z_ref[rows[k - 1], :] = jnp.dot(us[k - 1], win_ref[...], preferred_element_type=F32)
    z_ref[rows[-1], :] = jnp.dot(us[-1], win_ref[...], preferred_element_type=F32)


def _out_ffn_kernel(h_ref, mix_ref, ng_ref, wout_ref, wg_ref, wu_ref, wd_ref, y_ref,
                    splits=(D_FF,), before=(), after=()):
    m = jnp.dot(mix_ref[...].astype(BF16), wout_ref[...], preferred_element_type=F32)
    h = h_ref[...] + _rms(m, ng_ref[3:4, :])
    hn = _rms(h, ng_ref[4:5, :]).astype(BF16)
    y_ref[...] = h + 0.5 * _rms(_swiglu(hn, wg_ref, wu_ref, wd_ref, splits, before, after), ng_ref[5:6, :])


def _const_spec(shape):
    return pl.BlockSpec(shape, lambda *_: (0,) * len(shape), pipeline_mode=pl.Buffered(1))


def _layer_spec(l, shape, single_buffer=False):
    index_map = lambda *_: (l,) + (0,) * len(shape)
    if single_buffer:
        return pl.BlockSpec((None,) + shape, index_map, pipeline_mode=pl.Buffered(1))
    return pl.BlockSpec((None,) + shape, index_map)


PROMPT_FFN_TILES = BATCH * SEQ // TM_FFN
SAMPLE_FFN_TILES = DEC_BATCH * DEC_SEQ // TM_FFN


def _ffn_in(l, xp, xs, ng, wg, wu, wd, win, wout2, wg2, wu2, wd2):
    n = xp.shape[0] + xs.shape[0]
    row = lambda i: (i, 0)
    side_in, side_out, side_shape = [], [], []
    for w, blocks in ((wout2, SIDE_BLOCKS), (wg2, SIDE_BLOCKS), (wu2, SIDE_BLOCKS), (wd2, SIDE_BLOCKS_DOWN)):
        k_dim, n_dim = w.shape[1:]
        blk = (k_dim // blocks, n_dim)
        side_in.append(pl.BlockSpec((None,) + blk, lambda i, b=blocks: (l, jnp.minimum(i, b - 1), 0)))
        side_out.append(pl.BlockSpec(blk, lambda i, b=blocks: (jnp.minimum(i, b - 1), 0)))
        side_shape.append(jax.ShapeDtypeStruct((k_dim, n_dim), BF16))
    return pl.pallas_call(
        functools.partial(_ffn_in_kernel, layer=l),
        grid=(PROMPT_FFN_TILES + SAMPLE_FFN_TILES,),
        in_specs=[pl.BlockSpec((TM_FFN, D_MODEL), lambda i: (jnp.minimum(i, PROMPT_FFN_TILES - 1), 0)),
                  pl.BlockSpec((TM_FFN, D_MODEL), lambda i: (jnp.maximum(i - PROMPT_FFN_TILES, 0), 0)),
                  _layer_spec(l, (6, D_MODEL), True)] + [pl.BlockSpec(memory_space=pl.ANY)] * 4 + side_in,
        out_specs=[pl.BlockSpec((TM_FFN, D_MODEL), row), pl.BlockSpec((TM_FFN, IN_DIM), row)] + side_out,
        out_shape=[jax.ShapeDtypeStruct((n, D_MODEL), F32), jax.ShapeDtypeStruct((n, IN_DIM), F32)] + side_shape,
        scratch_shapes=[pltpu.VMEM((D_MODEL, D_FF), BF16), pltpu.VMEM((D_MODEL, D_FF), BF16),
                        pltpu.VMEM((D_FF, D_MODEL), BF16), pltpu.VMEM((D_MODEL, IN_DIM), BF16),
                        pltpu.VMEM((CAST_SLOTS, CAST_ROWS, D_FF), F32), pltpu.SemaphoreType.DMA((CAST_SLOTS,))],
        compiler_params=pltpu.CompilerParams(dimension_semantics=("arbitrary",),
                                             vmem_limit_bytes=VMEM_LIMIT_FUSED),
        name="ffn_in",
    )(xp, xs, ng, wg, wu, wd, win, wout2, wg2, wu2, wd2)


def _stack_q_heads(q):
    lane = lax.broadcasted_iota(jnp.int32, q.shape[:-1] + (LANES,), q.ndim - 1)
    pieces = []
    for h in range(N_HEADS):
        kvh = h // GROUP
        slab = q[..., (h // 2) * LANES:(h // 2 + 1) * LANES]
        if h % 2 != kvh:
            slab = pltpu.roll(slab, HEAD_DIM, q.ndim - 1)
        keep = (lane >= kvh * HEAD_DIM) & (lane < (kvh + 1) * HEAD_DIM)
        pieces.append((jnp.where(keep, slab, 0.0) * Q_SCALE2).astype(BF16))
    return jnp.concatenate(pieces, axis=-2)


def _unstack_heads(o_heads):
    ndim = o_heads[0].ndim
    lane = lax.broadcasted_iota(jnp.int32, o_heads[0].shape, ndim - 1)
    slabs = []
    for j in range(N_HEADS // 2):
        a, b = o_heads[2 * j], o_heads[2 * j + 1]
        kvh = (2 * j) // GROUP
        if kvh == 0:
            b = pltpu.roll(b, HEAD_DIM, ndim - 1)
        else:
            a = pltpu.roll(a, HEAD_DIM, ndim - 1)
        slabs.append(jnp.where(lane < HEAD_DIM, a, b))
    return jnp.concatenate(slabs, axis=-1)


def _alibi_bias(dist_f):
    return [ALIBI_SLOPES2[h] * dist_f for h in range(N_HEADS)]


def _softmax_heads(s, valid, bias, sinks, rows):
    es, rinv = [], []
    for h in range(N_HEADS):
        sh = s[..., h * rows:(h + 1) * rows, :]
        l = jnp.where(valid, sh - bias[h], NEG_INF)
        m = jnp.maximum(jnp.max(l, axis=-1, keepdims=True), sinks[h])
        e = jnp.exp2(l - m)
        den = jnp.sum(e, axis=-1, keepdims=True) + jnp.exp2(sinks[h] - m)
        es.append(e.astype(BF16))
        rinv.append(1.0 / den)
    return es, rinv


def _conv_ln_act(c, lng_ref, lnb_ref):
    mu = jnp.mean(c, axis=-1, keepdims=True)
    xc = c - mu
    y = xc * lax.rsqrt(jnp.mean(xc * xc, axis=-1, keepdims=True) + LN_EPS) * lng_ref[...] + lnb_ref[...]
    return (y * jax.nn.sigmoid(y)).astype(BF16)


def _pool_diff(sums, cnt, cur):
    lane = lax.broadcasted_iota(jnp.int32, cur.shape, cur.ndim - 1)
    wsum = sums[-1]
    for g in range(N_POOL_GROUPS - 2, -1, -1):
        wsum = jnp.where(lane < (g + 1) * POOL_GROUP_DIM, sums[g], wsum)
    return (wsum / cnt - cur).astype(BF16)


def _pool_window_lane():
    lane = lax.broadcasted_iota(jnp.int32, (1, POOL_DIM), 1)
    w = jnp.full((1, POOL_DIM), POOL_WINDOWS[-1], jnp.int32)
    for g in range(N_POOL_GROUPS - 2, -1, -1):
        w = jnp.where(lane < (g + 1) * POOL_GROUP_DIM, POOL_WINDOWS[g], w)
    return w


def _dw_conv_rows(gx_ref, dww_ref, r0, rows):
    first = CONV_HALO - (CONV_K - 1)
    outs = []
    for c0 in range(0, CONV_DIM, LANES):
        cols = slice(c0, c0 + LANES)
        out = None
        for r in range(SUBLANES):
            ext = rows if r == 0 else rows + SUBLANES
            acc = None
            for m in range((first + CONV_K - 1) // SUBLANES + 1):
                j = SUBLANES * m + r - first
                if j < 0 or j >= CONV_K:
                    continue
                lo = r0 + SUBLANES * m
                term = dww_ref[j:j + 1, cols] * gx_ref[lo:lo + ext, cols]
                acc = term if acc is None else acc + term
            part = acc if r == 0 else acc[r:r + rows, :]
            out = part if out is None else out + part
        outs.append(out)
    return jnp.concatenate(outs, axis=1)


def _prompt_mixer_pieces(i, sinks_ref, z_ref, dww_ref, dwb_ref, lng_ref, lnb_ref, pw_ref, poolw_ref, pools_ref,
                         mix_ref, nk_ref, nv_ref, nc_ref, np_ref,
                         kx_ref, vx_ref, gx_ref, px_ref, layer):
    tm = TM_MIX

    kx_ref[WINDOW:, :] = z_ref[:, K_OFF:K_OFF + KV_DIM].astype(BF16)
    vx_ref[WINDOW:, :] = z_ref[:, V_OFF:V_OFF + KV_DIM].astype(BF16)
    gx_ref[CONV_HALO:, :] = z_ref[:, GA_OFF:GA_OFF + CONV_DIM] * jax.nn.sigmoid(z_ref[:, GB_OFF:GB_OFF + CONV_DIM])
    px_ref[POOL_HALO:, :] = z_ref[:, PU_OFF:PU_OFF + POOL_DIM]

    q_io = lax.broadcasted_iota(jnp.int32, (Q_BLK, WINDOW + Q_BLK), 0)
    s_io = lax.broadcasted_iota(jnp.int32, (Q_BLK, WINDOW + Q_BLK), 1)
    dist = q_io + WINDOW - s_io
    band = (dist >= 0) & (dist < WINDOW)
    bias = _alibi_bias(dist.astype(F32))
    sinks = [sinks_ref[layer, h] * LOG2E for h in range(N_HEADS)]
    w_lane = _pool_window_lane()

    def head(blk):
        r0 = blk * Q_BLK
        first_key_pos = i * tm + r0 - WINDOW
        valid = band & (s_io + first_key_pos >= 0)
        qs = _stack_q_heads(z_ref[r0:r0 + Q_BLK, Q_OFF:Q_OFF + ATTN_DIM])
        kb = kx_ref[r0:r0 + WINDOW + Q_BLK, :]
        s = lax.dot_general(qs, kb, (((1,), (1,)), ((), ())), preferred_element_type=F32)
        es, rinv = _softmax_heads(s, valid, bias, sinks, Q_BLK)
        ys = []
        for rc in range(Q_BLK // CONV_ROWS):
            c = _dw_conv_rows(gx_ref, dww_ref, r0 + rc * CONV_ROWS, CONV_ROWS) + dwb_ref[...]
            ys.append(_conv_ln_act(c, lng_ref, lnb_ref))
        run = px_ref[r0:r0 + POOL_HALO + Q_BLK, :]
        cur = run[POOL_HALO:, :]
        sums = []
        span = 1
        while span < POOL_MAX:
            run = run + pltpu.roll(run, span, 0)
            span *= 2
            if span in POOL_WINDOWS:
                sums.append(run[POOL_HALO:, :])
        pos = i * tm + r0 + lax.broadcasted_iota(jnp.int32, (Q_BLK, 1), 0)
        cnt = jnp.minimum(w_lane, pos + 1).astype(F32)
        return (jnp.concatenate(es, axis=0), rinv, jnp.concatenate(ys, axis=0), _pool_diff(sums, cnt, cur))

    def tail(blk, p, rinv, y, d):
        r0 = blk * Q_BLK
        o = jnp.dot(p, vx_ref[r0:r0 + WINDOW + Q_BLK, :], preferred_element_type=F32)
        o_heads = [o[h * Q_BLK:(h + 1) * Q_BLK, :] * rinv[h] for h in range(N_HEADS)]
        mix_ref[r0:r0 + Q_BLK, 0:ATTN_DIM] = _unstack_heads(o_heads).astype(mix_ref.dtype)
        conv = jnp.dot(y, pw_ref[...], preferred_element_type=F32)
        mix_ref[r0:r0 + Q_BLK, CONV_OUT_OFF:CONV_OUT_OFF + CONV_DIM] = conv.astype(mix_ref.dtype)
        pool = jnp.dot(d, poolw_ref[...], preferred_element_type=F32) * pools_ref[...]
        mix_ref[r0:r0 + Q_BLK, POOL_OUT_OFF:POOL_OUT_OFF + POOL_DIM] = pool.astype(mix_ref.dtype)

    def finish():
        nk_ref[...] = z_ref[tm - WINDOW:, K_OFF:K_OFF + KV_DIM]
        nv_ref[...] = z_ref[tm - WINDOW:, V_OFF:V_OFF + KV_DIM]
        nc_ref[...] = gx_ref[CONV_HALO + tm - (CONV_K - 1):, :]
        np_ref[...] = px_ref[POOL_HALO + tm - (POOL_MAX - 1):, :]
        kx_ref[0:WINDOW, :] = kx_ref[tm:tm + WINDOW, :]
        vx_ref[0:WINDOW, :] = vx_ref[tm:tm + WINDOW, :]
        gx_ref[0:CONV_HALO, :] = gx_ref[tm:tm + CONV_HALO, :]
        px_ref[0:POOL_HALO, :] = px_ref[tm:tm + POOL_HALO, :]

    pending = {}

    def run_head(blk):
        pending[blk] = head(blk)

    def run_tail(blk):
        tail(blk, *pending.pop(blk))

    blocks = range(tm // Q_BLK)
    return ([functools.partial(run_head, blk) for blk in blocks],
            [functools.partial(run_tail, blk) for blk in blocks], finish)


def _mixer_weight_specs(l):
    return [pl.BlockSpec(memory_space=pltpu.SMEM),
            _layer_spec(l, (CONV_K, CONV_DIM)), _layer_spec(l, (1, CONV_DIM)),
            _layer_spec(l, (1, CONV_DIM)), _layer_spec(l, (1, CONV_DIM)),
            _layer_spec(l, (CONV_DIM, CONV_DIM)), _layer_spec(l, (POOL_DIM, POOL_DIM)),
            _layer_spec(l, (1, POOL_DIM))]


PROMPT_TILES = BATCH * SEQ // TM_MIX
TILES_PER_SEQ = SEQ // TM_MIX


def _prompt_mix_out_ffn_kernel(sinks_ref, z_ref, h_ref, dww_ref, dwb_ref, lng_ref, lnb_ref, pw_ref, poolw_ref,
                               pools_ref, ng_ref, wout_ref, wg_ref, wu_ref, wd_ref,
                               y_ref, nk_ref, nv_ref, nc_ref, np_ref,
                               kx_ref, vx_ref, gx_ref, px_ref, mix_ref, h2_ref, hn_ref, *, layer):
    s = pl.program_id(0)
    i = lax.rem(jnp.minimum(s, PROMPT_TILES - 1), TILES_PER_SEQ)

    @pl.when(i == 0)
    def _():
        kx_ref[0:WINDOW, :] = jnp.zeros((WINDOW, KV_DIM), BF16)
        vx_ref[0:WINDOW, :] = jnp.zeros((WINDOW, KV_DIM), BF16)
        gx_ref[0:CONV_HALO, :] = jnp.zeros((CONV_HALO, CONV_DIM), F32)
        px_ref[0:POOL_HALO, :] = jnp.zeros((POOL_HALO, POOL_DIM), F32)

    @pl.when(s == 0)
    def _():
        h2_ref[...] = jnp.zeros(h2_ref.shape, h2_ref.dtype)
        hn_ref[...] = jnp.zeros(hn_ref.shape, hn_ref.dtype)

    heads, tails, finish = _prompt_mixer_pieces(i, sinks_ref, z_ref, dww_ref, dwb_ref, lng_ref, lnb_ref, pw_ref,
                                                poolw_ref, pools_ref, mix_ref, nk_ref, nv_ref, nc_ref, np_ref,
                                                kx_ref, vx_ref, gx_ref, px_ref, layer)
    ffn = _swiglu(hn_ref[...], wg_ref, wu_ref, wd_ref, FF_SPLITS_FUSED, [None] + heads, tails)
    finish()
    y_ref[...] = h2_ref[...] + 0.5 * _rms(ffn, ng_ref[5:6, :])
    m = jnp.dot(mix_ref[...], wout_ref[...], preferred_element_type=F32)
    h2 = h_ref[...] + _rms(m, ng_ref[3:4, :])
    h2_ref[...] = h2
    hn_ref[...] = _rms(h2, ng_ref[4:5, :]).astype(BF16)


def _prompt_mix_out_ffn(l, z, h, sinks, dww, dwb, lng, lnb, pw, poolw, pools, ng, wout, wg, wu, wd):
    last = PROMPT_TILES - 1
    cur = lambda s: (jnp.minimum(s, last), 0)
    prev = lambda s: (jnp.maximum(s - 1, 0), 0)
    seq3 = lambda s: (jnp.minimum(s, last) // TILES_PER_SEQ, 0, 0)
    wspecs = _mixer_weight_specs(l)
    return pl.pallas_call(
        functools.partial(_prompt_mix_out_ffn_kernel, layer=l),
        grid=(PROMPT_TILES + 1,),
        in_specs=[wspecs[0], pl.BlockSpec((TM_MIX, IN_DIM), cur), pl.BlockSpec((TM_MIX, D_MODEL), cur)]
                 + wspecs[1:]
                 + [_layer_spec(l, (6, D_MODEL), True), _const_spec((MIX_DIM, D_MODEL)),
                    _const_spec((D_MODEL, D_FF)), _const_spec((D_MODEL, D_FF)),
                    _const_spec((D_FF, D_MODEL))],
        out_specs=[pl.BlockSpec((TM_MIX, D_MODEL), prev),
                   pl.BlockSpec((None, WINDOW, KV_DIM), seq3), pl.BlockSpec((None, WINDOW, KV_DIM), seq3),
                   pl.BlockSpec((None, CONV_K - 1, CONV_DIM), seq3),
                   pl.BlockSpec((None, POOL_MAX - 1, POOL_DIM), seq3)],
        out_shape=[jax.ShapeDtypeStruct((BATCH * SEQ, D_MODEL), F32),
                   jax.ShapeDtypeStruct((BATCH, WINDOW, KV_DIM), F32),
                   jax.ShapeDtypeStruct((BATCH, WINDOW, KV_DIM), F32),
                   jax.ShapeDtypeStruct((BATCH, CONV_K - 1, CONV_DIM), F32),
                   jax.ShapeDtypeStruct((BATCH, POOL_MAX - 1, POOL_DIM), F32)],
        scratch_shapes=[pltpu.VMEM((WINDOW + TM_MIX, KV_DIM), BF16),
                        pltpu.VMEM((WINDOW + TM_MIX, KV_DIM), BF16),
                        pltpu.VMEM((CONV_HALO + TM_MIX, CONV_DIM), F32),
                        pltpu.VMEM((POOL_HALO + TM_MIX, POOL_DIM), F32),
                        pltpu.VMEM((TM_MIX, MIX_DIM), BF16),
                        pltpu.VMEM((TM_MIX, D_MODEL), F32),
                        pltpu.VMEM((TM_MIX, D_MODEL), BF16)],
        compiler_params=pltpu.CompilerParams(dimension_semantics=("arbitrary",),
                                             vmem_limit_bytes=VMEM_LIMIT_FUSED),
        name="prompt_mix_out_ffn",
    )(sinks, z, h, dww, dwb, lng, lnb, pw, poolw, pools, ng, wout, wg, wu, wd)


def _sample_mixer_kernel(sinks_ref, z3_ref, z2_ref, sk_ref, sv_ref, sc_ref, sp_ref,
                         dww_ref, dwb_ref, lng_ref, lnb_ref, pw_ref, poolw_ref, pools_ref,
                         ck_ref, cv_ref, cc_ref, cp_ref,
                         mix_ref, nk_ref, nv_ref, nc_ref, np_ref,
                         kx_ref, vx_ref, gx_ref, px_ref, re_ref, *, layer):
    del ck_ref, cv_ref, cc_ref, cp_ref
    nb, t = SEQ_BLK, DEC_SEQ
    lane = lax.broadcasted_iota(jnp.int32, (KV_DIM, LANES), 1)

    for x_ref, s_ref, n_ref, off in ((kx_ref, sk_ref, nk_ref, K_OFF), (vx_ref, sv_ref, nv_ref, V_OFF)):
        new_t = z2_ref[:, off:off + KV_DIM].T
        for b in range(nb):
            old = jnp.concatenate([s_ref[b, kv] for kv in range(N_KV_HEADS)], axis=0)
            x_ref[b, :, 0:WINDOW] = old.astype(BF16)
            fresh = pltpu.roll(new_t, (LANES - b * t) % LANES, 1)
            x_ref[b, :, WINDOW:] = jnp.where(lane < t, fresh, 0.0).astype(BF16)
            shifted = jnp.where(lane < WINDOW - t, pltpu.roll(old, WINDOW - t, 1),
                                pltpu.roll(new_t, (WINDOW - t - b * t) % LANES, 1))
            for kv in range(N_KV_HEADS):
                n_ref[b, kv] = shifted[kv * HEAD_DIM:(kv + 1) * HEAD_DIM, :]

    q_io = lax.broadcasted_iota(jnp.int32, (t, S_KEYS), 0)
    s_io = lax.broadcasted_iota(jnp.int32, (t, S_KEYS), 1)
    dist = q_io + WINDOW - s_io
    valid = (dist >= 0) & (dist < WINDOW) & (s_io + (PAST_LEN - WINDOW) >= 0)
    bias = _alibi_bias(dist.astype(F32))
    sinks = [sinks_ref[layer, h] * LOG2E for h in range(N_HEADS)]

    qs = _stack_q_heads(z3_ref[:, :, Q_OFF:Q_OFF + ATTN_DIM])
    s = jnp.einsum('bqd,bdk->bqk', qs, kx_ref[...], preferred_element_type=F32)
    es, rinv = _softmax_heads(s, valid, bias, sinks, t)
    o = jnp.einsum('bqk,bdk->bqd', jnp.concatenate(es, axis=1), vx_ref[...], preferred_element_type=F32)
    o_heads = [o[:, h * t:(h + 1) * t, :] * rinv[h] for h in range(N_HEADS)]
    attn = _unstack_heads(o_heads)
    mix_ref[:, 0:ATTN_DIM] = attn.reshape(nb * t, ATTN_DIM)

    def to_token_major(x, dst_ref, first):
        for cb in range(x.shape[1] // LANES):
            re_ref[cb] = x[:, cb * LANES:(cb + 1) * LANES]
        for tok in range(t):
            for cb in range(x.shape[1] // LANES):
                dst_ref[first + tok, :, cb * LANES:(cb + 1) * LANES] = re_ref[cb, pl.ds(tok, nb, stride=t), :]

    def to_seq_major(x):
        for tok in range(t):
            for cb in range(x.shape[1] // LANES):
                re_ref[cb, pl.ds(tok, nb, stride=t), :] = x[tok * nb:(tok + 1) * nb, cb * LANES:(cb + 1) * LANES]
        return jnp.concatenate([re_ref[cb] for cb in range(x.shape[1] // LANES)], axis=1)

    hist = CONV_K - 1
    gx_ref[0:hist] = sc_ref[...]
    to_token_major(z2_ref[:, GA_OFF:GA_OFF + CONV_DIM] * jax.nn.sigmoid(z2_ref[:, GB_OFF:GB_OFF + CONV_DIM]),
                   gx_ref, hist)
    nc_ref[...] = gx_ref[t:t + hist]
    ys = []
    for tok in range(t):
        c = jnp.broadcast_to(dwb_ref[...], (nb, CONV_DIM))
        for j in range(CONV_K):
            c = c + dww_ref[j:j + 1, :] * gx_ref[tok + j]
        ys.append(_conv_ln_act(c, lng_ref, lnb_ref))
    conv = jnp.dot(jnp.concatenate(ys, axis=0), pw_ref[...], preferred_element_type=F32)
    mix_ref[:, CONV_OUT_OFF:CONV_OUT_OFF + CONV_DIM] = to_seq_major(conv)

    hist = POOL_MAX - 1
    px_ref[0:hist] = sp_ref[...]
    to_token_major(z2_ref[:, PU_OFF:PU_OFF + POOL_DIM], px_ref, hist)
    np_ref[...] = px_ref[t:t + hist]
    w_lane = _pool_window_lane()
    ds = []
    for tok in range(t):
        cur = px_ref[hist + tok]
        run = cur
        sums = []
        for back in range(1, POOL_MAX):
            run = run + px_ref[hist + tok - back]
            if back + 1 in POOL_WINDOWS:
                sums.append(run)
        cnt = jnp.minimum(w_lane, PAST_LEN + tok + 1).astype(F32)
        ds.append(_pool_diff(sums, cnt, cur))
    pool = jnp.dot(jnp.concatenate(ds, axis=0), poolw_ref[...], preferred_element_type=F32) * pools_ref[...]
    mix_ref[:, POOL_OUT_OFF:POOL_OUT_OFF + POOL_DIM] = to_seq_major(pool)


def _sample_mix_out_ffn_kernel(sinks_ref, z3_ref, z2_ref, sk_ref, sv_ref, sc_ref, sp_ref,
                               dww_ref, dwb_ref, lng_ref, lnb_ref, pw_ref, poolw_ref, pools_ref,
                               ck_ref, cv_ref, cc_ref, cp_ref, h_ref, ng_ref, wout_ref, wg_ref, wu_ref, wd_ref,
                               y_ref, nk_ref, nv_ref, nc_ref, np_ref,
                               kx_ref, vx_ref, gx_ref, px_ref, re_ref, mix_ref, *, layer):
    _sample_mixer_kernel(sinks_ref, z3_ref, z2_ref, sk_ref, sv_ref, sc_ref, sp_ref,
                         dww_ref, dwb_ref, lng_ref, lnb_ref, pw_ref, poolw_ref, pools_ref,
                         ck_ref, cv_ref, cc_ref, cp_ref, mix_ref, nk_ref, nv_ref, nc_ref, np_ref,
                         kx_ref, vx_ref, gx_ref, px_ref, re_ref, layer=layer)
    _out_ffn_kernel(h_ref, mix_ref, ng_ref, wout_ref, wg_ref, wu_ref, wd_ref, y_ref, FF_SPLITS)


def _sample_mix_out_ffn(l, z, h, sk, sv, sc, sp, new_states, sinks, dww, dwb, lng, lnb, pw, poolw, pools,
                        ng, wout, wg, wu, wd):
    nb = SEQ_BLK
    rows = nb * DEC_SEQ
    assert rows == LANES and S_KEYS == 2 * LANES and WINDOW == LANES
    kv_spec = pl.BlockSpec((None, nb, N_KV_HEADS, HEAD_DIM, WINDOW), lambda j: (l, j, 0, 0, 0))
    rows_spec = lambda r, ch: pl.BlockSpec((None, r, nb, ch), lambda j: (l, 0, j, 0))
    wspecs = _mixer_weight_specs(l)
    first = BATCH * SEQ // rows
    operands = (sinks, z.reshape(-1, DEC_SEQ, IN_DIM), z, sk, sv, sc, sp, dww, dwb, lng, lnb, pw, poolw, pools)
    carried = tuple(new_states)
    return pl.pallas_call(
        functools.partial(_sample_mix_out_ffn_kernel, layer=l),
        grid=(DEC_BATCH // nb,),
        in_specs=[wspecs[0],
                  pl.BlockSpec((nb, DEC_SEQ, IN_DIM), lambda j: (first + j, 0, 0)),
                  pl.BlockSpec((rows, IN_DIM), lambda j: (first + j, 0)),
                  kv_spec, kv_spec, rows_spec(CONV_K - 1, CONV_DIM), rows_spec(POOL_MAX - 1, POOL_DIM)]
                 + wspecs[1:] + [pl.BlockSpec(memory_space=pl.ANY)] * len(carried)
                 + [pl.BlockSpec((rows, D_MODEL), lambda j: (first + j, 0)),
                    _layer_spec(l, (6, D_MODEL), True), _const_spec((MIX_DIM, D_MODEL)),
                    _const_spec((D_MODEL, D_FF)), _const_spec((D_MODEL, D_FF)),
                    _const_spec((D_FF, D_MODEL))],
        out_specs=[pl.BlockSpec((rows, D_MODEL), lambda j: (j, 0)),
                   kv_spec, kv_spec, rows_spec(CONV_K - 1, CONV_DIM), rows_spec(POOL_MAX - 1, POOL_DIM)],
        out_shape=[jax.ShapeDtypeStruct((DEC_BATCH * DEC_SEQ, D_MODEL), F32),
                   jax.ShapeDtypeStruct(sk.shape, F32), jax.ShapeDtypeStruct(sv.shape, F32),
                   jax.ShapeDtypeStruct(sc.shape, F32), jax.ShapeDtypeStruct(sp.shape, F32)],
        input_output_aliases={len(operands) + n: 1 + n for n in range(len(carried))},
        scratch_shapes=[pltpu.VMEM((nb, KV_DIM, S_KEYS), BF16),
                        pltpu.VMEM((nb, KV_DIM, S_KEYS), BF16),
                        pltpu.VMEM((CONV_K - 1 + DEC_SEQ, nb, CONV_DIM), F32),
                        pltpu.VMEM((POOL_MAX - 1 + DEC_SEQ, nb, POOL_DIM), F32),
                        pltpu.VMEM((max(CONV_DIM, POOL_DIM) // LANES, rows, LANES), F32),
                        pltpu.VMEM((rows, MIX_DIM), F32)],
        compiler_params=pltpu.CompilerParams(dimension_semantics=("arbitrary",),
                                             vmem_limit_bytes=VMEM_LIMIT),
        name="sample_mix_out_ffn",
    )(*operands, *carried, h, ng, wout, wg, wu, wd)


def _block_diag(pool_w):
    eye = jnp.eye(N_POOL_GROUPS, dtype=pool_w.dtype)
    return jnp.einsum('lgcd,gh->lgchd', pool_w, eye).reshape(DEPTH, POOL_DIM, POOL_DIM)


def kernel(x_prompt, x_sample, state_attn_k, state_attn_v, state_conv, state_pool, norm_g, ffn1_wg, ffn1_wu, ffn1_wd, w_in, attn_sinks, conv_dw_w, conv_dw_b, conv_ln_g, conv_ln_b, conv_pw_w, pool_w, pool_scale, w_out, ffn2_wg, ffn2_wu, ffn2_wd):
    hp = x_prompt.reshape(BATCH * SEQ, D_MODEL)
    hs = x_sample.reshape(DEC_BATCH * DEC_SEQ, D_MODEL)
    sk_all = state_attn_k.transpose(0, 1, 3, 4, 2)
    sv_all = state_attn_v.transpose(0, 1, 3, 4, 2)
    sc_all = state_conv.transpose(0, 2, 1, 3)
    sp_all = state_pool.transpose(0, 2, 1, 3)
    f1 = (ffn1_wg, ffn1_wu, ffn1_wd)
    win = w_in
    mixer_w = (attn_sinks, conv_dw_w, conv_dw_b.reshape(DEPTH, 1, CONV_DIM),
               conv_ln_g.reshape(DEPTH, 1, CONV_DIM), conv_ln_b.reshape(DEPTH, 1, CONV_DIM),
               conv_pw_w.astype(BF16), _block_diag(pool_w).astype(BF16),
               pool_scale.reshape(DEPTH, 1, POOL_DIM))
    outs = [[] for _ in range(4)]
    sample_states = [jnp.zeros_like(a) for a in (sk_all, sv_all, sc_all, sp_all)]
    for l in range(DEPTH):
        h1, z, *w2 = _ffn_in(l, hp, hs, norm_g, *f1, win, w_out, ffn2_wg, ffn2_wu, ffn2_wd)
        hp, k1, v1, c1, p1 = _prompt_mix_out_ffn(l, z, h1, *mixer_w, norm_g, *w2)
        hs, *sample_states = _sample_mix_out_ffn(l, z, h1, sk_all, sv_all, sc_all, sp_all, sample_states,
                                                 *mixer_w, norm_g, *w2)
        for lst, val in zip(outs, (k1, v1, c1, p1)):
            lst.append(val)

    kv_p = (DEPTH, BATCH, WINDOW, N_KV_HEADS, HEAD_DIM)
    k2, v2, c2, p2 = sample_states
    return (hp.reshape(BATCH, SEQ, D_MODEL), hs.reshape(DEC_BATCH, DEC_SEQ, D_MODEL),
            jnp.stack(outs[0]).reshape(kv_p), jnp.stack(outs[1]).reshape(kv_p),
            jnp.stack(outs[2]), jnp.stack(outs[3]),
            k2.transpose(0, 1, 4, 2, 3), v2.transpose(0, 1, 4, 2, 3),
            c2.transpose(0, 2, 1, 3), p2.transpose(0, 2, 1, 3))
```
